```python
import math
import jax, jax.numpy as jnp
from jax import lax
import numpy as np

D_MODEL = 1024
BATCH = 4
SEQ = 4096
DEPTH = 4

ATTN_WIDTH = D_MODEL // 2
CONV_CHANNELS = D_MODEL - ATTN_WIDTH
V_HEAD_DIM = 64
N_HEADS = ATTN_WIDTH // V_HEAD_DIM
QK_NOPE_DIM = 64
QK_ROPE_DIM = 32
QK_HEAD_DIM = QK_NOPE_DIM + QK_ROPE_DIM
Q_LORA_RANK = 384
KV_LORA_RANK = 256
CONV_GROUPS = 8
CONV_KERNEL = 3
D_FF = 4 * D_MODEL
PLE_DIM = 256
ROPE_THETA = 10000.0
Q_BLOCK = 128
EPS = 1e-6
IN_SPLITS = (Q_LORA_RANK, KV_LORA_RANK, QK_ROPE_DIM, CONV_CHANNELS, CONV_CHANNELS, CONV_CHANNELS)
IN_WIDTH = sum(IN_SPLITS)

kernel_name = "hybrid_mla_shortconv_sandwich_encoder"


def rms_norm(x, g):
    xf = x.astype(jnp.float32)
    y = xf * lax.rsqrt(jnp.mean(xf * xf, axis=-1, keepdims=True) + EPS)
    return (y * g.astype(jnp.float32)).astype(x.dtype)


def group_rms_norm(x, g, n_groups):
    b, s, w = x.shape
    y = rms_norm(x.reshape(b, s, n_groups, w // n_groups), g.reshape(n_groups, w // n_groups))
    return y.reshape(b, s, w)


def rope_tables(seq):
    inv_freq = 1.0 / (ROPE_THETA ** (jnp.arange(0, QK_ROPE_DIM, 2, dtype=jnp.float32) / QK_ROPE_DIM))
    ang = jnp.arange(seq, dtype=jnp.float32)[:, None] * inv_freq[None, :]
    return jnp.cos(ang), jnp.sin(ang)


def apply_rope(x, cos, sin):
    half = QK_ROPE_DIM // 2
    c = cos.astype(x.dtype)
    s = sin.astype(x.dtype)
    x1, x2 = x[..., :half], x[..., half:]
    return jnp.concatenate([x1 * c - x2 * s, x2 * c + x1 * s], axis=-1)


def latent_attention(q_nope, q_rope, k_nope, k_rope, v):
    b, s, h, _ = q_nope.shape
    nb = s // Q_BLOCK
    qn = q_nope.reshape(b, nb, Q_BLOCK, h, QK_NOPE_DIM).transpose(1, 0, 2, 3, 4)
    qr = q_rope.reshape(b, nb, Q_BLOCK, h, QK_ROPE_DIM).transpose(1, 0, 2, 3, 4)
    scale = 1.0 / math.sqrt(QK_HEAD_DIM)

    def one_block(args):
        qn_b, qr_b = args
        sc = (jnp.einsum('bqhd,bkhd->bhqk', qn_b, k_nope, preferred_element_type=jnp.float32)
              + jnp.einsum('bqhr,bkr->bhqk', qr_b, k_rope, preferred_element_type=jnp.float32))
        w = jax.nn.softmax(sc * scale, axis=-1)
        return jnp.einsum('bhqk,bkhd->bqhd', w.astype(v.dtype), v)

    o = lax.map(one_block, (qn, qr))
    return o.transpose(1, 0, 2, 3, 4).reshape(b, s, h * V_HEAD_DIM)


def short_conv(u, w):
    return lax.conv_general_dilated(
        u, w[:, None, :].astype(u.dtype), window_strides=(1,), padding=((1, 1),),
        dimension_numbers=('NWC', 'WIO', 'NWC'), feature_group_count=u.shape[-1])


def setup_inputs(seed: int = 0) -> dict:
    key = jax.random.key(seed)
    ks = jax.random.split(key, 24)
    f32 = jnp.float32

    def w(k, shape, fan_in):
        return jax.random.normal(k, shape, f32) * (fan_in ** -0.5)

    def gain(k, dim):
        return 1.0 + 0.02 * jax.random.normal(k, (DEPTH, dim), f32)

    return {
        "x": jax.random.normal(ks[0], (BATCH, SEQ, D_MODEL), f32),
        "p": jax.random.normal(ks[1], (DEPTH, BATCH, SEQ, PLE_DIM), f32),
        "g_pre_mix": gain(ks[2], D_MODEL),
        "w_in": w(ks[3], (DEPTH, D_MODEL, IN_WIDTH), D_MODEL),
        "g_q_a": gain(ks[4], Q_LORA_RANK),
        "w_q_b": w(ks[5], (DEPTH, Q_LORA_RANK, N_HEADS * QK_HEAD_DIM), Q_LORA_RANK),
        "g_kv_a": gain(ks[6], KV_LORA_RANK),
        "w_kv_b": w(ks[7], (DEPTH, KV_LORA_RANK, N_HEADS * (QK_NOPE_DIM + V_HEAD_DIM)), KV_LORA_RANK),
        "w_conv": w(ks[8], (DEPTH, CONV_KERNEL, CONV_CHANNELS), CONV_KERNEL),
        "g_attn_out": gain(ks[9], ATTN_WIDTH),
        "g_conv_out": gain(ks[10], CONV_CHANNELS),
        "w_out": w(ks[11], (DEPTH, ATTN_WIDTH + CONV_CHANNELS, D_MODEL), ATTN_WIDTH + CONV_CHANNELS),
        "g_post_mix": gain(ks[12], D_MODEL),
        "g_pre_mlp": gain(ks[13], D_MODEL),
        "w_mlp_up": w(ks[14], (DEPTH, D_MODEL, D_FF), D_MODEL),
        "w_mlp_down": w(ks[15], (DEPTH, D_FF, D_MODEL), D_FF),
        "g_post_mlp": gain(ks[16], D_MODEL),
        "g_pre_ple": gain(ks[17], D_MODEL),
        "w_ple_gate": w(ks[18], (DEPTH, D_MODEL, D_MODEL), D_MODEL),
        "w_ple_proj": w(ks[19], (DEPTH, PLE_DIM, D_MODEL), PLE_DIM),
        "g_post_ple": gain(ks[20], D_MODEL),
    }


def reference(x, p, g_pre_mix, w_in, g_q_a, w_q_b, g_kv_a, w_kv_b, w_conv, g_attn_out,
              g_conv_out, w_out, g_post_mix, g_pre_mlp, w_mlp_up, w_mlp_down, g_post_mlp,
              g_pre_ple, w_ple_gate, w_ple_proj, g_post_ple):
    b, s, _ = x.shape
    cos, sin = rope_tables(s)
    split_at = list(np.cumsum(IN_SPLITS)[:-1])

    for i in range(DEPTH):
        h = rms_norm(x, g_pre_mix[i])
        z = h @ w_in[i]
        z_qa, z_kva, z_kr, gate_b, gate_c, conv_in = jnp.split(z, split_at, axis=-1)

        q = (rms_norm(z_qa, g_q_a[i]) @ w_q_b[i]).reshape(b, s, N_HEADS, QK_HEAD_DIM)
        q_nope = q[..., :QK_NOPE_DIM]
        q_rope = apply_rope(q[..., QK_NOPE_DIM:], cos[:, None, :], sin[:, None, :])
        kv = (rms_norm(z_kva, g_kv_a[i]) @ w_kv_b[i]).reshape(b, s, N_HEADS, QK_NOPE_DIM + V_HEAD_DIM)
        k_nope = kv[..., :QK_NOPE_DIM]
        v = kv[..., QK_NOPE_DIM:]
        k_rope = apply_rope(z_kr, cos, sin)
        attn = latent_attention(q_nope, q_rope, k_nope, k_rope, v)

        conv = gate_b * short_conv(gate_c * conv_in, w_conv[i])

        mix = jnp.concatenate([group_rms_norm(attn, g_attn_out[i], N_HEADS),
                               group_rms_norm(conv, g_conv_out[i], CONV_GROUPS)], axis=-1)
        x = x + rms_norm(mix @ w_out[i], g_post_mix[i])

        hm = rms_norm(x, g_pre_mlp[i])
        f = jnp.square(jax.nn.relu(hm @ w_mlp_up[i])) @ w_mlp_down[i]
        x = x + rms_norm(f, g_post_mlp[i])

        gate = jax.nn.sigmoid(rms_norm(x, g_pre_ple[i]) @ w_ple_gate[i])
        e = p[i] @ w_ple_proj[i]
        x = x + rms_norm(gate * e, g_post_ple[i])

    return x
```

```python
import functools
import math

import jax
import jax.numpy as jnp
from jax.experimental import pallas as pl
from jax.experimental.pallas import tpu as pltpu

D_MODEL = 1024
N_HEADS = 8
V_HEAD_DIM = 64
QK_NOPE_DIM = 64
QK_ROPE_DIM = 32
QK_HEAD_DIM = QK_NOPE_DIM + QK_ROPE_DIM
Q_LORA_RANK = 384
KV_LORA_RANK = 256
CONV_CHANNELS = 512
ATTN_WIDTH = 512
GROUP_WIDTH = 64
D_FF = 4096
PLE_DIM = 256
ROPE_THETA = 10000.0
EPS = 1e-6

LANES = 128
HEAD_PAD = LANES
HALO_ROWS = 8

C_QA = 0
C_KVA = C_QA + Q_LORA_RANK
C_KR = C_KVA + KV_LORA_RANK
C_KRSW = C_KR + HEAD_PAD
C_GB = C_KRSW + HEAD_PAD
C_GC = C_GB + CONV_CHANNELS
C_CI = C_GC + CONV_CHANNELS
IN_EXT = C_CI + CONV_CHANNELS

TM_IN = 512
TQ = 256
CK = 512
TM_TAIL = 512
FF_CHUNK = 1024
VMEM_LIMIT = 56 * 1024 * 1024


def _rms(x, g):
    return x * jax.lax.rsqrt(jnp.mean(x * x, axis=-1, keepdims=True) + EPS) * g


def _group_rms_128(c, g):
    lane = jax.lax.broadcasted_iota(jnp.int32, c.shape, 1)
    lo = lane < GROUP_WIDTH
    sq = c * c
    s0 = jnp.sum(jnp.where(lo, sq, 0.0), axis=-1, keepdims=True)
    s1 = jnp.sum(jnp.where(lo, 0.0, sq), axis=-1, keepdims=True)
    r0 = jax.lax.rsqrt(s0 * (1.0 / GROUP_WIDTH) + EPS)
    r1 = jax.lax.rsqrt(s1 * (1.0 / GROUP_WIDTH) + EPS)
    return c * jnp.where(lo, r0, r1) * g


def _dot(a, b):
    return jnp.dot(a, b, preferred_element_type=jnp.float32)


def _in_proj_kernel(x_ref, xp_ref, xn_ref, tab_ref, gpre_ref, win_ref, gqa_ref, wq_ref,
                    gkva_ref, wkv_ref, wconv_ref, gconv_ref,
                    q_ref, k_ref, v_ref, cn_ref):
    t = pl.program_id(1)
    nt = pl.num_programs(1)
    tm = x_ref.shape[1]
    gpre = gpre_ref[...]

    h = _rms(x_ref[0], gpre).astype(jnp.bfloat16)
    z_att = _dot(h, win_ref[:, :C_GB])
    z_cv = _dot(h, win_ref[:, C_GB:])

    tab = tab_ref[...]
    cq, sq = tab[:, 0:LANES], tab[:, LANES:2 * LANES]
    ck, sk = tab[:, 2 * LANES:3 * LANES], tab[:, 3 * LANES:4 * LANES]

    qa = _rms(z_att[:, C_QA:C_KVA], gqa_ref[...]).astype(jnp.bfloat16)
    qq = _dot(qa, wq_ref[...])
    for hd in range(N_HEADS):
        lo = hd * HEAD_PAD
        q_ref[0, hd] = (qq[:, lo:lo + HEAD_PAD] * cq
                        + qq[:, N_HEADS * HEAD_PAD + lo:N_HEADS * HEAD_PAD + lo + HEAD_PAD] * sq
                        ).astype(q_ref.dtype)

    kva = _rms(z_att[:, C_KVA:C_KR], gkva_ref[...]).astype(jnp.bfloat16)
    kv = _dot(kva, wkv_ref[...])
    krr = z_att[:, C_KR:C_KRSW] * ck + z_att[:, C_KRSW:C_GB] * sk
    for hd in range(N_HEADS):
        lo = hd * HEAD_PAD
        k_ref[0, hd] = (kv[:, lo:lo + HEAD_PAD] + krr).astype(k_ref.dtype)
    for j in range(N_HEADS // 2):
        lo = N_HEADS * HEAD_PAD + j * LANES
        v_ref[0, j] = kv[:, lo:lo + LANES].astype(v_ref.dtype)

    xh = jnp.concatenate([xp_ref[0], xn_ref[0]], axis=0)
    hh = _rms(xh, gpre).astype(jnp.bfloat16)
    zh = _dot(hh, win_ref[:, C_GC:])
    uh = zh[:, :CONV_CHANNELS] * zh[:, CONV_CHANNELS:]
    u_prev = jnp.where(t > 0, uh[HALO_ROWS - 1:HALO_ROWS], 0.0)
    u_next = jnp.where(t < nt - 1, uh[HALO_ROWS:HALO_ROWS + 1], 0.0)

    gb = z_cv[:, :CONV_CHANNELS]
    u = z_cv[:, CONV_CHANNELS:2 * CONV_CHANNELS] * z_cv[:, 2 * CONV_CHANNELS:]
    rows = jax.lax.broadcasted_iota(jnp.int32, u.shape, 0)
    u_dn = jnp.where(rows == 0, u_prev, pltpu.roll(u, 1, 0))
    u_up = jnp.where(rows == tm - 1, u_next, pltpu.roll(u, tm - 1, 0))
    wc = wconv_ref[...]
    cv = gb * (u_dn * wc[0:1] + u * wc[1:2] + u_up * wc[2:3])
    gconv = gconv_ref[...]
    for j in range(CONV_CHANNELS // LANES):
        sl = slice(j * LANES, (j + 1) * LANES)
        cn_ref[0, :, sl] = _group_rms_128(cv[:, sl], gconv[:, sl]).astype(cn_ref.dtype)


def _in_proj(x, tab, gpre, win, gqa, wq, gkva, wkv, wconv, gconv):
    b, s, d = x.shape
    tm = TM_IN
    nt = s // tm
    hb = tm // HALO_ROWS
    n_halo = s // HALO_ROWS
    const = lambda bi, ti: (0, 0)
    return pl.pallas_call(
        _in_proj_kernel,
        grid=(b, nt),
        in_specs=[
            pl.BlockSpec((1, tm, d), lambda bi, ti: (bi, ti, 0)),
            pl.BlockSpec((1, HALO_ROWS, d), lambda bi, ti: (bi, jnp.maximum(ti * hb - 1, 0), 0)),
            pl.BlockSpec((1, HALO_ROWS, d), lambda bi, ti: (bi, jnp.minimum((ti + 1) * hb, n_halo - 1), 0)),
            pl.BlockSpec((tm, 4 * LANES), lambda bi, ti: (ti, 0)),
            pl.BlockSpec(gpre.shape, const),
            pl.BlockSpec(win.shape, const),
            pl.BlockSpec(gqa.shape, const),
            pl.BlockSpec(wq.shape, const),
            pl.BlockSpec(gkva.shape, const),
            pl.BlockSpec(wkv.shape, const),
            pl.BlockSpec(wconv.shape, const),
            pl.BlockSpec(gconv.shape, const),
        ],
        out_specs=[
            pl.BlockSpec((1, N_HEADS, tm, HEAD_PAD), lambda bi, ti: (bi, 0, ti, 0)),
            pl.BlockSpec((1, N_HEADS, tm, HEAD_PAD), lambda bi, ti: (bi, 0, ti, 0)),
            pl.BlockSpec((1, N_HEADS // 2, tm, LANES), lambda bi, ti: (bi, 0, ti, 0)),
            pl.BlockSpec((1, tm, CONV_CHANNELS), lambda bi, ti: (bi, ti, 0)),
        ],
        out_shape=[
            jax.ShapeDtypeStruct((b, N_HEADS, s, HEAD_PAD), jnp.bfloat16),
            jax.ShapeDtypeStruct((b, N_HEADS, s, HEAD_PAD), jnp.bfloat16),
            jax.ShapeDtypeStruct((b, N_HEADS // 2, s, LANES), jnp.bfloat16),
            jax.ShapeDtypeStruct((b, s, CONV_CHANNELS), jnp.bfloat16),
        ],
        compiler_params=pltpu.CompilerParams(
            dimension_semantics=("arbitrary", "arbitrary"), vmem_limit_bytes=VMEM_LIMIT),
        name="in_proj",
    )(x, x, x, tab, gpre, win, gqa, wq, gkva, wkv, wconv, gconv)


def _attention_kernel(q_ref, k_ref, v_ref, g_ref, o_ref):
    tq = q_ref.shape[2]
    s_len = k_ref.shape[2]
    outs = []
    for hh in range(2):
        q = q_ref[0, hh]
        m = jnp.full((tq, 1), -jnp.inf, jnp.float32)
        l = jnp.zeros((tq, 1), jnp.float32)
        acc = jnp.zeros((tq, LANES), jnp.float32)
        for c in range(s_len // CK):
            kc = k_ref[0, hh, c * CK:(c + 1) * CK, :]
            s = jax.lax.dot_general(q, kc, (((1,), (1,)), ((), ())),
                                    preferred_element_type=jnp.float32)
            m_new = jnp.maximum(m, jnp.max(s, axis=-1, keepdims=True))
            alpha = jnp.exp2(m - m_new)
            p = jnp.exp2(s - m_new)
            l = alpha * l + jnp.sum(p, axis=-1, keepdims=True)
            acc = alpha * acc + _dot(p.astype(jnp.bfloat16), v_ref[0, 0, c * CK:(c + 1) * CK, :])
            m = m_new
        outs.append(acc / l)
    lane = jax.lax.broadcasted_iota(jnp.int32, (tq, LANES), 1)
    o = jnp.where(lane < GROUP_WIDTH, outs[0], outs[1])
    o_ref[0] = _group_rms_128(o, g_ref[...]).astype(o_ref.dtype)


def _attention(q, k, v, g_attn):
    b, nh, s, _ = q.shape
    return pl.pallas_call(
        _attention_kernel,
        grid=(b, nh // 2, s // TQ),
        in_specs=[
            pl.BlockSpec((1, 2, TQ, HEAD_PAD), lambda bi, j, qi: (bi, j, qi, 0)),
            pl.BlockSpec((1, 2, s, HEAD_PAD), lambda bi, j, qi: (bi, j, 0, 0)),
            pl.BlockSpec((1, 1, s, LANES), lambda bi, j, qi: (bi, j, 0, 0)),
            pl.BlockSpec((1, LANES), lambda bi, j, qi: (0, j)),
        ],
        out_specs=pl.BlockSpec((1, TQ, LANES), lambda bi, j, qi: (bi, qi, j)),
        out_shape=jax.ShapeDtypeStruct((b, s, ATTN_WIDTH), jnp.bfloat16),
        compiler_params=pltpu.CompilerParams(
            dimension_semantics=("arbitrary", "arbitrary", "arbitrary"), vmem_limit_bytes=VMEM_LIMIT),
        name="attention",
    )(q, k, v, g_attn)


def _tail_kernel(x_ref, an_ref, cn_ref, p_ref, wout_ref, gpost_ref, gpmlp_ref, wup_ref, wdn_ref,
                 gomlp_ref, gpple_ref, wg_ref, wp_ref, gople_ref, o_ref):
    mix = _dot(an_ref[0], wout_ref[:ATTN_WIDTH, :]) + _dot(cn_ref[0], wout_ref[ATTN_WIDTH:, :])
    x = x_ref[0] + _rms(mix, gpost_ref[...])

    hm = _rms(x, gpmlp_ref[...]).astype(jnp.bfloat16)
    f = jnp.zeros(x.shape, jnp.float32)
    for j in range(D_FF // FF_CHUNK):
        sl = slice(j * FF_CHUNK, (j + 1) * FF_CHUNK)
        a = jnp.maximum(_dot(hm, wup_ref[:, sl]), 0.0)
        f = f + _dot((a * a).astype(jnp.bfloat16), wdn_ref[sl, :])
    x = x + _rms(f, gomlp_ref[...])

    hg = _rms(x, gpple_ref[...]).astype(jnp.bfloat16)
    gate = jax.nn.sigmoid(_dot(hg, wg_ref[...]))
    e = _dot(p_ref[0].astype(jnp.bfloat16), wp_ref[...])
    o_ref[0] = x + _rms(gate * e, gople_ref[...])


def _tail(x, an, cn, p, wout, gpost, gpmlp, wup, wdn, gomlp, gpple, wg, wp, gople):
    b, s, d = x.shape
    tm = TM_TAIL
    const = lambda bi, ti: (0, 0)
    tile = lambda w: pl.BlockSpec((1, tm, w), lambda bi, ti: (bi, ti, 0))
    resident = lambda a: pl.BlockSpec(a.shape, const, pipeline_mode=pl.Buffered(1))
    return pl.pallas_call(
        _tail_kernel,
        grid=(b, s // tm),
        in_specs=[tile(d), tile(ATTN_WIDTH), tile(CONV_CHANNELS), tile(PLE_DIM),
                  resident(wout), resident(gpost), resident(gpmlp), resident(wup), resident(wdn),
                  resident(gomlp), resident(gpple), resident(wg), resident(wp), resident(gople)],
        out_specs=tile(d),
        out_shape=jax.ShapeDtypeStruct(x.shape, x.dtype),
        compiler_params=pltpu.CompilerParams(
            dimension_semantics=("arbitrary", "arbitrary"), vmem_limit_bytes=VMEM_LIMIT),
        name="tail",
    )(x, an, cn, p, wout, gpost, gpmlp, wup, wdn, gomlp, gpple, wg, wp, gople)


def _rope_tables(seq):
    inv_freq = 1.0 / (ROPE_THETA ** (jnp.arange(0, QK_ROPE_DIM, 2, dtype=jnp.float32) / QK_ROPE_DIM))
    ang = jnp.arange(seq, dtype=jnp.float32)[:, None] * inv_freq[None, :]
    cos, sin = jnp.cos(ang), jnp.sin(ang)
    c = math.log2(math.e) / math.sqrt(QK_HEAD_DIM)
    ones = jnp.ones((seq, QK_NOPE_DIM), jnp.float32)
    z64 = jnp.zeros((seq, QK_NOPE_DIM), jnp.float32)
    z32 = jnp.zeros((seq, HEAD_PAD - QK_HEAD_DIM), jnp.float32)
    cq = c * jnp.concatenate([ones, cos, cos, z32], axis=1)
    sq = c * jnp.concatenate([z64, sin, sin, z32], axis=1)
    ck = jnp.concatenate([z64, cos, cos, z32], axis=1)
    sk = jnp.concatenate([z64, sin, sin, z32], axis=1)
    return jnp.concatenate([cq, sq, ck, sk], axis=1)


def _prep_w_in(w_in):
    depth, d, _ = w_in.shape
    half = QK_ROPE_DIM // 2
    o_kr = Q_LORA_RANK + KV_LORA_RANK
    x1 = w_in[:, :, o_kr:o_kr + half]
    x2 = w_in[:, :, o_kr + half:o_kr + QK_ROPE_DIM]
    z64 = jnp.zeros((depth, d, QK_NOPE_DIM), w_in.dtype)
    z32 = jnp.zeros((depth, d, HEAD_PAD - QK_HEAD_DIM), w_in.dtype)
    kr = jnp.concatenate([z64, x1, x2, z32], axis=-1)
    krsw = jnp.concatenate([z64, -x2, x1, z32], axis=-1)
    ext = jnp.concatenate([w_in[:, :, :o_kr], kr, krsw, w_in[:, :, o_kr + QK_ROPE_DIM:]], axis=-1)
    return ext.astype(jnp.bfloat16)


def _prep_w_q(w_q_b):
    depth, r, _ = w_q_b.shape
    half = QK_ROPE_DIM // 2
    w = w_q_b.reshape(depth, r, N_HEADS, QK_HEAD_DIM)
    nope = w[..., :QK_NOPE_DIM]
    r1 = w[..., QK_NOPE_DIM:QK_NOPE_DIM + half]
    r2 = w[..., QK_NOPE_DIM + half:]
    z64 = jnp.zeros_like(nope)
    z32 = jnp.zeros(w.shape[:-1] + (HEAD_PAD - QK_HEAD_DIM,), w.dtype)
    main = jnp.concatenate([nope, r1, r2, z32], axis=-1).reshape(depth, r, N_HEADS * HEAD_PAD)
    swap = jnp.concatenate([z64, -r2, r1, z32], axis=-1).reshape(depth, r, N_HEADS * HEAD_PAD)
    return jnp.concatenate([main, swap], axis=-1).astype(jnp.bfloat16)


def _prep_w_kv(w_kv_b):
    depth, r, _ = w_kv_b.shape
    w = w_kv_b.reshape(depth, r, N_HEADS, QK_NOPE_DIM + V_HEAD_DIM)
    kn = w[..., :QK_NOPE_DIM]
    v = w[..., QK_NOPE_DIM:]
    kpad = jnp.concatenate([kn, jnp.zeros_like(kn)], axis=-1).reshape(depth, r, N_HEADS * HEAD_PAD)
    vcat = v.reshape(depth, r, N_HEADS * V_HEAD_DIM)
    return jnp.concatenate([kpad, vcat], axis=-1).astype(jnp.bfloat16)


def kernel(x, p, g_pre_mix, w_in, g_q_a, w_q_b, g_kv_a, w_kv_b, w_conv, g_attn_out, g_conv_out, w_out, g_post_mix, g_pre_mlp, w_mlp_up, w_mlp_down, g_post_mlp, g_pre_ple, w_ple_gate, w_ple_proj, g_post_ple):
    depth = w_in.shape[0]
    s = x.shape[1]
    tab = _rope_tables(s)
    win = _prep_w_in(w_in)
    wq = _prep_w_q(w_q_b)
    wkv = _prep_w_kv(w_kv_b)
    bf = lambda w: w.astype(jnp.bfloat16)
    wout, wup, wdn, wg, wp = bf(w_out), bf(w_mlp_up), bf(w_mlp_down), bf(w_ple_gate), bf(w_ple_proj)
    row = lambda g, i: g[i][None, :]

    for i in range(depth):
        q, k, v, cn = _in_proj(x, tab, row(g_pre_mix, i), win[i], row(g_q_a, i), wq[i],
                               row(g_kv_a, i), wkv[i], w_conv[i], row(g_conv_out, i))
        an = _attention(q, k, v, row(g_attn_out, i))
        x = _tail(x, an, cn, p[i], wout[i], row(g_post_mix, i), row(g_pre_mlp, i), wup[i], wdn[i],
                  row(g_post_mlp, i), row(g_pre_ple, i), wg[i], wp[i], row(g_post_ple, i))
    return x
```

```python
import functools
import math

import jax
import jax.numpy as jnp
from jax.experimental import pallas as pl
from jax.experimental.pallas import tpu as pltpu

D_MODEL = 1024
N_HEADS = 8
V_HEAD_DIM = 64
QK_NOPE_DIM = 64
QK_ROPE_DIM = 32
QK_HEAD_DIM = QK_NOPE_DIM + QK_ROPE_DIM
Q_LORA_RANK = 384
KV_LORA_RANK = 256
CONV_CHANNELS = 512
ATTN_WIDTH = 512
GROUP_WIDTH = 64
D_FF = 4096
PLE_DIM = 256
ROPE_THETA = 10000.0
EPS = 1e-6

LANES = 128
SUBLANES = 8
ONES_ROWS = 16
HEAD_PAD = LANES
HALO_ROWS = 8

C_QA = 0
C_KVA = C_QA + Q_LORA_RANK
C_KR = C_KVA + KV_LORA_RANK
C_KRSW = C_KR + HEAD_PAD
C_GB = C_KRSW + HEAD_PAD
C_GC = C_GB + CONV_CHANNELS
C_CI = C_GC + CONV_CHANNELS
IN_EXT = C_CI + CONV_CHANNELS

TM_IN = 512
TQ = 256
CK = 512
TM_TAIL = 512
FF_CHUNK = 1024
VMEM_LIMIT = 56 * 1024 * 1024


def _rms(x, g):
    return x * jax.lax.rsqrt(jnp.mean(x * x, axis=-1, keepdims=True) + EPS) * g


def _group_rms_128(c, g):
    lane = jax.lax.broadcasted_iota(jnp.int32, c.shape, 1)
    lo = lane < GROUP_WIDTH
    sq = c * c
    s0 = jnp.sum(jnp.where(lo, sq, 0.0), axis=-1, keepdims=True)
    s1 = jnp.sum(jnp.where(lo, 0.0, sq), axis=-1, keepdims=True)
    r0 = jax.lax.rsqrt(s0 * (1.0 / GROUP_WIDTH) + EPS)
    r1 = jax.lax.rsqrt(s1 * (1.0 / GROUP_WIDTH) + EPS)
    return c * jnp.where(lo, r0, r1) * g


def _dot(a, b):
    return jnp.dot(a, b, preferred_element_type=jnp.float32)


def _in_proj_kernel(x_ref, xp_ref, xn_ref, tab_ref, gpre_ref, win_ref, gqa_ref, wq_ref,
                    gkva_ref, wkv_ref, wconv_ref, gconv_ref,
                    q_ref, k_ref, v_ref, cn_ref):
    t = pl.program_id(1)
    nt = pl.num_programs(1)
    tm = x_ref.shape[1]
    gpre = gpre_ref[...]

    h = _rms(x_ref[0], gpre).astype(jnp.bfloat16)
    z_att = _dot(h, win_ref[:, :C_GB])
    z_cv = _dot(h, win_ref[:, C_GB:])

    tab = tab_ref[...]
    cq, sq = tab[:, 0:LANES], tab[:, LANES:2 * LANES]
    ck, sk = tab[:, 2 * LANES:3 * LANES], tab[:, 3 * LANES:4 * LANES]

    qa = _rms(z_att[:, C_QA:C_KVA], gqa_ref[...]).astype(jnp.bfloat16)
    qq = _dot(qa, wq_ref[...])
    for hd in range(N_HEADS):
        lo = hd * HEAD_PAD
        q_ref[0, hd] = (qq[:, lo:lo + HEAD_PAD] * cq
                        + qq[:, N_HEADS * HEAD_PAD + lo:N_HEADS * HEAD_PAD + lo + HEAD_PAD] * sq
                        ).astype(q_ref.dtype)

    kva = _rms(z_att[:, C_KVA:C_KR], gkva_ref[...]).astype(jnp.bfloat16)
    kv = _dot(kva, wkv_ref[...])
    krr = z_att[:, C_KR:C_KRSW] * ck + z_att[:, C_KRSW:C_GB] * sk
    for hd in range(N_HEADS):
        lo = hd * HEAD_PAD
        k_ref[0, hd] = (kv[:, lo:lo + HEAD_PAD] + krr).astype(k_ref.dtype)
    for j in range(N_HEADS // 2):
        lo = N_HEADS * HEAD_PAD + j * LANES
        v_ref[0, j] = kv[:, lo:lo + LANES].T.astype(v_ref.dtype)

    xh = jnp.concatenate([xp_ref[0], xn_ref[0]], axis=0)
    hh = _rms(xh, gpre).astype(jnp.bfloat16)
    zh = _dot(hh, win_ref[:, C_GC:])
    uh = zh[:, :CONV_CHANNELS] * zh[:, CONV_CHANNELS:]
    u_prev = jnp.where(t > 0, uh[HALO_ROWS - 1:HALO_ROWS], 0.0)
    u_next = jnp.where(t < nt - 1, uh[HALO_ROWS:HALO_ROWS + 1], 0.0)

    gb = z_cv[:, :CONV_CHANNELS]
    u = z_cv[:, CONV_CHANNELS:2 * CONV_CHANNELS] * z_cv[:, 2 * CONV_CHANNELS:]
    rows = jax.lax.broadcasted_iota(jnp.int32, u.shape, 0)
    u_dn = jnp.where(rows == 0, u_prev, pltpu.roll(u, 1, 0))
    u_up = jnp.where(rows == tm - 1, u_next, pltpu.roll(u, tm - 1, 0))
    wc = wconv_ref[...]
    cv = gb * (u_dn * wc[0:1] + u * wc[1:2] + u_up * wc[2:3])
    gconv = gconv_ref[...]
    for j in range(CONV_CHANNELS // LANES):
        sl = slice(j * LANES, (j + 1) * LANES)
        cn_ref[0, :, sl] = _group_rms_128(cv[:, sl], gconv[:, sl]).astype(cn_ref.dtype)


def _in_proj(x, tab, gpre, win, gqa, wq, gkva, wkv, wconv, gconv):
    b, s, d = x.shape
    tm = TM_IN
    nt = s // tm
    hb = tm // HALO_ROWS
    n_halo = s // HALO_ROWS
    const = lambda bi, ti: (0, 0)
    return pl.pallas_call(
        _in_proj_kernel,
        grid=(b, nt),
        in_specs=[
            pl.BlockSpec((1, tm, d), lambda bi, ti: (bi, ti, 0)),
            pl.BlockSpec((1, HALO_ROWS, d), lambda bi, ti: (bi, jnp.maximum(ti * hb - 1, 0), 0)),
            pl.BlockSpec((1, HALO_ROWS, d), lambda bi, ti: (bi, jnp.minimum((ti + 1) * hb, n_halo - 1), 0)),
            pl.BlockSpec((tm, 4 * LANES), lambda bi, ti: (ti, 0)),
            pl.BlockSpec(gpre.shape, const),
            pl.BlockSpec(win.shape, const),
            pl.BlockSpec(gqa.shape, const),
            pl.BlockSpec(wq.shape, const),
            pl.BlockSpec(gkva.shape, const),
            pl.BlockSpec(wkv.shape, const),
            pl.BlockSpec(wconv.shape, const),
            pl.BlockSpec(gconv.shape, const),
        ],
        out_specs=[
            pl.BlockSpec((1, N_HEADS, tm, HEAD_PAD), lambda bi, ti: (bi, 0, ti, 0)),
            pl.BlockSpec((1, N_HEADS, tm, HEAD_PAD), lambda bi, ti: (bi, 0, ti, 0)),
            pl.BlockSpec((1, N_HEADS // 2, LANES, tm), lambda bi, ti: (bi, 0, 0, ti)),
            pl.BlockSpec((1, tm, CONV_CHANNELS), lambda bi, ti: (bi, ti, 0)),
        ],
        out_shape=[
            jax.ShapeDtypeStruct((b, N_HEADS, s, HEAD_PAD), jnp.bfloat16),
            jax.ShapeDtypeStruct((b, N_HEADS, s, HEAD_PAD), jnp.bfloat16),
            jax.ShapeDtypeStruct((b, N_HEADS // 2, LANES, s), jnp.bfloat16),
            jax.ShapeDtypeStruct((b, s, CONV_CHANNELS), jnp.bfloat16),
        ],
        compiler_params=pltpu.CompilerParams(
            dimension_semantics=("arbitrary", "arbitrary"), vmem_limit_bytes=VMEM_LIMIT),
        name="in_proj",
    )(x, x, x, tab, gpre, win, gqa, wq, gkva, wkv, wconv, gconv)


def _attention_kernel(q_ref, k_ref, v_ref, g_ref, o_ref, s_ref, m_ref):
    tq = q_ref.shape[2]
    s_len = k_ref.shape[2]
    n_chunks = s_len // CK

    @pl.when(pl.program_id(0) == 0)
    def _():
        s_ref[...] = jnp.zeros(s_ref.shape, s_ref.dtype)
        m_ref[...] = jnp.zeros(m_ref.shape, m_ref.dtype)

    ones = jnp.ones((ONES_ROWS, CK), jnp.bfloat16)
    m_prev = [m_ref[hh] for hh in range(2)]
    q = [q_ref[0, hh] for hh in range(2)]
    acc = [jnp.zeros((V_HEAD_DIM + ONES_ROWS, tq), jnp.float32) for _ in range(2)]
    m8 = [None, None]
    for c in range(n_chunks):
        rows = slice(c * CK, (c + 1) * CK)
        for hh in range(2):
            pt = jnp.exp2(s_ref[hh, rows, :] - m_prev[hh]).astype(jnp.bfloat16)
            vt = v_ref[0, 0, hh * V_HEAD_DIM:(hh + 1) * V_HEAD_DIM, rows]
            acc[hh] = acc[hh] + _dot(jnp.concatenate([vt, ones], axis=0), pt)
        for hh in range(2):
            st = jax.lax.dot_general(k_ref[0, hh, rows, :], q[hh], (((1,), (1,)), ((), ())),
                                     preferred_element_type=jnp.float32)
            s_ref[hh, rows, :] = st
            cm = jnp.max(st.reshape(CK // SUBLANES, SUBLANES, tq), axis=0)
            m8[hh] = cm if m8[hh] is None else jnp.maximum(m8[hh], cm)
    outs = []
    for hh in range(2):
        m_ref[hh] = jnp.max(m8[hh], axis=0, keepdims=True)
        o = acc[hh][:V_HEAD_DIM] / acc[hh][V_HEAD_DIM:V_HEAD_DIM + 1]
        ssq = jnp.sum(o * o, axis=0, keepdims=True)
        outs.append(o * jax.lax.rsqrt(ssq * (1.0 / V_HEAD_DIM) + EPS))
    o_pair = jnp.concatenate(outs, axis=0).T
    o_ref[0] = (o_pair * g_ref[...]).astype(o_ref.dtype)


def _attention(q, k, v, g_attn):
    b, nh, s, _ = q.shape
    n_pairs = nh // 2
    nq = s // TQ
    n_tiles = b * n_pairs * nq

    def tile_index(t):
        return t // (n_pairs * nq), (t // nq) % n_pairs, t % nq

    def scored(t):
        return tile_index(jnp.minimum(t, n_tiles - 1))

    def finished(t):
        return tile_index(jnp.maximum(t - 1, 0))

    def q_map(t):
        bi, j, qi = scored(t)
        return bi, j, qi, 0

    def k_map(t):
        bi, j, _ = scored(t)
        return bi, j, 0, 0

    def v_map(t):
        bi, j, _ = finished(t)
        return bi, j, 0, 0

    def g_map(t):
        return 0, finished(t)[1]

    def o_map(t):
        bi, j, qi = finished(t)
        return bi, qi, j

    return pl.pallas_call(
        _attention_kernel,
        grid=(n_tiles + 1,),
        in_specs=[
            pl.BlockSpec((1, 2, TQ, HEAD_PAD), q_map),
            pl.BlockSpec((1, 2, s, HEAD_PAD), k_map),
            pl.BlockSpec((1, 1, LANES, s), v_map),
            pl.BlockSpec((1, LANES), g_map),
        ],
        out_specs=pl.BlockSpec((1, TQ, LANES), o_map),
        out_shape=jax.ShapeDtypeStruct((b, s, ATTN_WIDTH), jnp.bfloat16),
        scratch_shapes=[pltpu.VMEM((2, s, TQ), jnp.float32), pltpu.VMEM((2, 1, TQ), jnp.float32)],
        compiler_params=pltpu.CompilerParams(
            dimension_semantics=("arbitrary",), vmem_limit_bytes=VMEM_LIMIT),
        name="attention",
    )(q, k, v, g_attn)


def _tail_kernel(x_ref, an_ref, cn_ref, p_ref, wout_ref, gpost_ref, gpmlp_ref, wup_ref, wdn_ref,
                 gomlp_ref, gpple_ref, wg_ref, wp_ref, gople_ref, o_ref):
    mix = _dot(an_ref[0], wout_ref[:ATTN_WIDTH, :]) + _dot(cn_ref[0], wout_ref[ATTN_WIDTH:, :])
    x = x_ref[0] + _rms(mix, gpost_ref[...])

    hm = _rms(x, gpmlp_ref[...]).astype(jnp.bfloat16)
    f = jnp.zeros(x.shape, jnp.float32)
    for j in range(D_FF // FF_CHUNK):
        sl = slice(j * FF_CHUNK, (j + 1) * FF_CHUNK)
        a = jnp.maximum(_dot(hm, wup_ref[:, sl]), 0.0)
        f = f + _dot((a * a).astype(jnp.bfloat16), wdn_ref[sl, :])
    x = x + _rms(f, gomlp_ref[...])

    hg = _rms(x, gpple_ref[...]).astype(jnp.bfloat16)
    gate = jax.nn.sigmoid(_dot(hg, wg_ref[...]))
    e = _dot(p_ref[0].astype(jnp.bfloat16), wp_ref[...])
    o_ref[0] = x + _rms(gate * e, gople_ref[...])


def _tail(x, an, cn, p, wout, gpost, gpmlp, wup, wdn, gomlp, gpple, wg, wp, gople):
    b, s, d = x.shape
    tm = TM_TAIL
    const = lambda bi, ti: (0, 0)
    tile = lambda w: pl.BlockSpec((1, tm, w), lambda bi, ti: (bi, ti, 0))
    resident = lambda a: pl.BlockSpec(a.shape, const, pipeline_mode=pl.Buffered(1))
    return pl.pallas_call(
        _tail_kernel,
        grid=(b, s // tm),
        in_specs=[tile(d), tile(ATTN_WIDTH), tile(CONV_CHANNELS), tile(PLE_DIM),
                  resident(wout), resident(gpost), resident(gpmlp), resident(wup), resident(wdn),
                  resident(gomlp), resident(gpple), resident(wg), resident(wp), resident(gople)],
        out_specs=tile(d),
        out_shape=jax.ShapeDtypeStruct(x.shape, x.dtype),
        compiler_params=pltpu.CompilerParams(
            dimension_semantics=("arbitrary", "arbitrary"), vmem_limit_bytes=VMEM_LIMIT),
        name="tail",
    )(x, an, cn, p, wout, gpost, gpmlp, wup, wdn, gomlp, gpple, wg, wp, gople)


def _rope_tables(seq):
    inv_freq = 1.0 / (ROPE_THETA ** (jnp.arange(0, QK_ROPE_DIM, 2, dtype=jnp.float32) / QK_ROPE_DIM))
    ang = jnp.arange(seq, dtype=jnp.float32)[:, None] * inv_freq[None, :]
    cos, sin = jnp.cos(ang), jnp.sin(ang)
    c = math.log2(math.e) / math.sqrt(QK_HEAD_DIM)
    ones = jnp.ones((seq, QK_NOPE_DIM), jnp.float32)
    z64 = jnp.zeros((seq, QK_NOPE_DIM), jnp.float32)
    z32 = jnp.zeros((seq, HEAD_PAD - QK_HEAD_DIM), jnp.float32)
    cq = c * jnp.concatenate([ones, cos, cos, z32], axis=1)
    sq = c * jnp.concatenate([z64, sin, sin, z32], axis=1)
    ck = jnp.concatenate([z64, cos, cos, z32], axis=1)
    sk = jnp.concatenate([z64, sin, sin, z32], axis=1)
    return jnp.concatenate([cq, sq, ck, sk], axis=1)


def _prep_w_in(w_in):
    depth, d, _ = w_in.shape
    half = QK_ROPE_DIM // 2
    o_kr = Q_LORA_RANK + KV_LORA_RANK
    x1 = w_in[:, :, o_kr:o_kr + half]
    x2 = w_in[:, :, o_kr + half:o_kr + QK_ROPE_DIM]
    z64 = jnp.zeros((depth, d, QK_NOPE_DIM), w_in.dtype)
    z32 = jnp.zeros((depth, d, HEAD_PAD - QK_HEAD_DIM), w_in.dtype)
    kr = jnp.concatenate([z64, x1, x2, z32], axis=-1)
    krsw = jnp.concatenate([z64, -x2, x1, z32], axis=-1)
    ext = jnp.concatenate([w_in[:, :, :o_kr], kr, krsw, w_in[:, :, o_kr + QK_ROPE_DIM:]], axis=-1)
    return ext.astype(jnp.bfloat16)


def _prep_w_q(w_q_b):
    depth, r, _ = w_q_b.shape
    half = QK_ROPE_DIM // 2
    w = w_q_b.reshape(depth, r, N_HEADS, QK_HEAD_DIM)
    nope = w[..., :QK_NOPE_DIM]
    r1 = w[..., QK_NOPE_DIM:QK_NOPE_DIM + half]
    r2 = w[..., QK_NOPE_DIM + half:]
    z64 = jnp.zeros_like(nope)
    z32 = jnp.zeros(w.shape[:-1] + (HEAD_PAD - QK_HEAD_DIM,), w.dtype)
    main = jnp.concatenate([nope, r1, r2, z32], axis=-1).reshape(depth, r, N_HEADS * HEAD_PAD)
    swap = jnp.concatenate([z64, -r2, r1, z32], axis=-1).reshape(depth, r, N_HEADS * HEAD_PAD)
    return jnp.concatenate([main, swap], axis=-1).astype(jnp.bfloat16)


def _prep_w_kv(w_kv_b):
    depth, r, _ = w_kv_b.shape
    w = w_kv_b.reshape(depth, r, N_HEADS, QK_NOPE_DIM + V_HEAD_DIM)
    kn = w[..., :QK_NOPE_DIM]
    v = w[..., QK_NOPE_DIM:]
    kpad = jnp.concatenate([kn, jnp.zeros_like(kn)], axis=-1).reshape(depth, r, N_HEADS * HEAD_PAD)
    vcat = v.reshape(depth, r, N_HEADS * V_HEAD_DIM)
    return jnp.concatenate([kpad, vcat], axis=-1).astype(jnp.bfloat16)


def kernel(x, p, g_pre_mix, w_in, g_q_a, w_q_b, g_kv_a, w_kv_b, w_conv, g_attn_out, g_conv_out, w_out, g_post_mix, g_pre_mlp, w_mlp_up, w_mlp_down, g_post_mlp, g_pre_ple, w_ple_gate, w_ple_proj, g_post_ple):
    depth = w_in.shape[0]
    s = x.shape[1]
    tab = _rope_tables(s)
    win = _prep_w_in(w_in)
    wq = _prep_w_q(w_q_b)
    wkv = _prep_w_kv(w_kv_b)
    bf = lambda w: w.astype(jnp.bfloat16)
    wout, wup, wdn, wg, wp = bf(w_out), bf(w_mlp_up), bf(w_mlp_down), bf(w_ple_gate), bf(w_ple_proj)
    row = lambda g, i: g[i][None, :]

    for i in range(depth):
        q, k, v, cn = _in_proj(x, tab, row(g_pre_mix, i), win[i], row(g_q_a, i), wq[i],
                               row(g_kv_a, i), wkv[i], w_conv[i], row(g_conv_out, i))
        an = _attention(q, k, v, row(g_attn_out, i))
        x = _tail(x, an, cn, p[i], wout[i], row(g_post_mix, i), row(g_pre_mlp, i), wup[i], wdn[i],
                  row(g_post_mlp, i), row(g_pre_ple, i), wg[i], wp[i], row(g_post_ple, i))
    return x
```

```python
import functools
import math

import jax
import jax.numpy as jnp
from jax.experimental import pallas as pl
from jax.experimental.pallas import tpu as pltpu

D_MODEL = 1024
N_HEADS = 8
V_HEAD_DIM = 64
QK_NOPE_DIM = 64
QK_ROPE_DIM = 32
QK_HEAD_DIM = QK_NOPE_DIM + QK_ROPE_DIM
Q_LORA_RANK = 384
KV_LORA_RANK = 256
CONV_CHANNELS = 512
ATTN_WIDTH = 512
GROUP_WIDTH = 64
D_FF = 4096
PLE_DIM = 256
ROPE_THETA = 10000.0
EPS = 1e-6

LANES = 128
SUBLANES = 8
ONES_ROWS = 16
HEAD_PAD = LANES
N_ROPE_LAYOUTS = LANES // QK_ROPE_DIM
QK_SCALE = math.log2(math.e) / math.sqrt(QK_HEAD_DIM)
HALO_ROWS = 8

C_QA = 0
C_KVA = C_QA + Q_LORA_RANK
C_KR = C_KVA + KV_LORA_RANK
C_GB = C_KR + HEAD_PAD
C_GC = C_GB + CONV_CHANNELS
C_CI = C_GC + CONV_CHANNELS
IN_EXT = C_CI + CONV_CHANNELS

TM_IN = 512
TQ = 512
CK = 512
TM_TAIL = 512
FF_CHUNK = 1024
VMEM_LIMIT = 56 * 1024 * 1024


def _rms(x, g):
    return x * jax.lax.rsqrt(jnp.mean(x * x, axis=-1, keepdims=True) + EPS) * g


def _group_rms_128(c, g):
    lane = jax.lax.broadcasted_iota(jnp.int32, c.shape, 1)
    lo = lane < GROUP_WIDTH
    sq = c * c
    s0 = jnp.sum(jnp.where(lo, sq, 0.0), axis=-1, keepdims=True)
    s1 = jnp.sum(jnp.where(lo, 0.0, sq), axis=-1, keepdims=True)
    r0 = jax.lax.rsqrt(s0 * (1.0 / GROUP_WIDTH) + EPS)
    r1 = jax.lax.rsqrt(s1 * (1.0 / GROUP_WIDTH) + EPS)
    return c * jnp.where(lo, r0, r1) * g


def _dot(a, b):
    return jnp.dot(a, b, preferred_element_type=jnp.float32)


def _rope_window(layout):
    lane = jax.lax.broadcasted_iota(jnp.int32, (1, LANES), 1)
    lo = layout * QK_ROPE_DIM
    return (lane >= lo) & (lane < lo + QK_ROPE_DIM)


def _in_proj_kernel(x_ref, xp_ref, xn_ref, tab_ref, gpre_ref, win_ref, gqa_ref, wq_ref,
                    gkva_ref, wkv_ref, wconv_ref, gconv_ref,
                    q_ref, k_ref, v_ref, cn_ref):
    t = pl.program_id(1)
    nt = pl.num_programs(1)
    tm = x_ref.shape[1]
    gpre = gpre_ref[...]

    h = _rms(x_ref[0], gpre).astype(jnp.bfloat16)
    z_att = _dot(h, win_ref[:, :C_GB])
    z_cv = _dot(h, win_ref[:, C_GB:])

    tab = tab_ref[...]
    cos_q, sin_q = tab[:, 0:LANES], tab[:, LANES:2 * LANES]
    cos_k, sin_k = tab[:, 2 * LANES:3 * LANES], tab[:, 3 * LANES:4 * LANES]
    windows = [_rope_window(r) for r in range(N_ROPE_LAYOUTS)]

    qa = _rms(z_att[:, C_QA:C_KVA], gqa_ref[...]).astype(jnp.bfloat16)
    qq = _dot(qa, wq_ref[...])
    q_cos = [jnp.where(w, cos_q, QK_SCALE) for w in windows]
    q_sin = [jnp.where(w, sin_q, 0.0) for w in windows]
    for hd in range(N_HEADS):
        r = hd % N_ROPE_LAYOUTS
        main = qq[:, hd * HEAD_PAD:(hd + 1) * HEAD_PAD]
        lo = N_HEADS * HEAD_PAD + (hd // N_ROPE_LAYOUTS) * LANES
        q_ref[0, hd] = (main * q_cos[r] + qq[:, lo:lo + LANES] * q_sin[r]).astype(q_ref.dtype)

    kva = _rms(z_att[:, C_KVA:C_KR], gkva_ref[...]).astype(jnp.bfloat16)
    kv = _dot(kva, wkv_ref[...])
    zkr = z_att[:, C_KR:C_GB]
    krr = zkr * cos_k + pltpu.roll(zkr, QK_ROPE_DIM // 2, 1) * sin_k
    kr_win = [jnp.where(w, krr, 0.0) for w in windows]
    for hd in range(N_HEADS):
        lo = hd * HEAD_PAD
        k_ref[0, hd] = (kv[:, lo:lo + HEAD_PAD] + kr_win[hd % N_ROPE_LAYOUTS]).astype(k_ref.dtype)
    for j in range(N_HEADS // 2):
        lo = N_HEADS * HEAD_PAD + j * LANES
        v_ref[0, j] = kv[:, lo:lo + LANES].T.astype(v_ref.dtype)

    xh = jnp.concatenate([xp_ref[0], xn_ref[0]], axis=0)
    hh = _rms(xh, gpre).astype(jnp.bfloat16)
    zh = _dot(hh, win_ref[:, C_GC:])
    uh = zh[:, :CONV_CHANNELS] * zh[:, CONV_CHANNELS:]
    u_prev = jnp.where(t > 0, uh[HALO_ROWS - 1:HALO_ROWS], 0.0)
    u_next = jnp.where(t < nt - 1, uh[HALO_ROWS:HALO_ROWS + 1], 0.0)

    gb = z_cv[:, :CONV_CHANNELS]
    u = z_cv[:, CONV_CHANNELS:2 * CONV_CHANNELS] * z_cv[:, 2 * CONV_CHANNELS:]
    rows = jax.lax.broadcasted_iota(jnp.int32, u.shape, 0)
    u_dn = jnp.where(rows == 0, u_prev, pltpu.roll(u, 1, 0))
    u_up = jnp.where(rows == tm - 1, u_next, pltpu.roll(u, tm - 1, 0))
    wc = wconv_ref[...]
    cv = gb * (u_dn * wc[0:1] + u * wc[1:2] + u_up * wc[2:3])
    gconv = gconv_ref[...]
    for j in range(CONV_CHANNELS // LANES):
        sl = slice(j * LANES, (j + 1) * LANES)
        cn_ref[0, :, sl] = _group_rms_128(cv[:, sl], gconv[:, sl]).astype(cn_ref.dtype)


def _layer_spec(a, layer):
    return pl.BlockSpec((None,) + a.shape[1:], lambda *_: (layer, 0, 0))


def _in_proj(layer, x, tab, gpre, win, gqa, wq, gkva, wkv, wconv, gconv):
    b, s, d = x.shape
    tm = TM_IN
    nt = s // tm
    hb = tm // HALO_ROWS
    n_halo = s // HALO_ROWS
    return pl.pallas_call(
        _in_proj_kernel,
        grid=(b, nt),
        in_specs=[
            pl.BlockSpec((1, tm, d), lambda bi, ti: (bi, ti, 0)),
            pl.BlockSpec((1, HALO_ROWS, d), lambda bi, ti: (bi, jnp.maximum(ti * hb - 1, 0), 0)),
            pl.BlockSpec((1, HALO_ROWS, d), lambda bi, ti: (bi, jnp.minimum((ti + 1) * hb, n_halo - 1), 0)),
            pl.BlockSpec((tm, tab.shape[1]), lambda bi, ti: (ti, 0)),
        ] + [_layer_spec(a, layer) for a in (gpre, win, gqa, wq, gkva, wkv, wconv, gconv)],
        out_specs=[
            pl.BlockSpec((1, N_HEADS, tm, HEAD_PAD), lambda bi, ti: (bi, 0, ti, 0)),
            pl.BlockSpec((1, N_HEADS, tm, HEAD_PAD), lambda bi, ti: (bi, 0, ti, 0)),
            pl.BlockSpec((1, N_HEADS // 2, LANES, tm), lambda bi, ti: (bi, 0, 0, ti)),
            pl.BlockSpec((1, tm, CONV_CHANNELS), lambda bi, ti: (bi, ti, 0)),
        ],
        out_shape=[
            jax.ShapeDtypeStruct((b, N_HEADS, s, HEAD_PAD), jnp.bfloat16),
            jax.ShapeDtypeStruct((b, N_HEADS, s, HEAD_PAD), jnp.bfloat16),
            jax.ShapeDtypeStruct((b, N_HEADS // 2, LANES, s), jnp.bfloat16),
            jax.ShapeDtypeStruct((b, s, CONV_CHANNELS), jnp.bfloat16),
        ],
        compiler_params=pltpu.CompilerParams(
            dimension_semantics=("arbitrary", "arbitrary"), vmem_limit_bytes=VMEM_LIMIT),
        name="in_proj",
    )(x, x, x, tab, gpre, win, gqa, wq, gkva, wkv, wconv, gconv)


def _attention_kernel(q_ref, k_ref, v_ref, g_ref, o_ref, s_ref, m_ref):
    tq = q_ref.shape[2]
    s_len = k_ref.shape[2]
    n_chunks = s_len // CK

    @pl.when(pl.program_id(0) == 0)
    def _():
        s_ref[...] = jnp.zeros(s_ref.shape, s_ref.dtype)
        m_ref[...] = jnp.zeros(m_ref.shape, m_ref.dtype)

    ones = jnp.ones((ONES_ROWS, CK), jnp.bfloat16)
    m_prev = [m_ref[hh] for hh in range(2)]
    q = [q_ref[0, hh] for hh in range(2)]
    acc = [jnp.zeros((V_HEAD_DIM + ONES_ROWS, tq), jnp.float32) for _ in range(2)]
    m8 = [None, None]
    for c in range(n_chunks):
        rows = slice(c * CK, (c + 1) * CK)
        for hh in range(2):
            pt = jnp.exp2(s_ref[hh, rows, :] - m_prev[hh]).astype(jnp.bfloat16)
            vt = v_ref[0, 0, hh * V_HEAD_DIM:(hh + 1) * V_HEAD_DIM, rows]
            acc[hh] = acc[hh] + _dot(jnp.concatenate([vt, ones], axis=0), pt)
        for hh in range(2):
            st = jax.lax.dot_general(k_ref[0, hh, rows, :], q[hh], (((1,), (1,)), ((), ())),
                                     preferred_element_type=jnp.float32)
            s_ref[hh, rows, :] = st
            cm = jnp.max(st.reshape(CK // SUBLANES, SUBLANES, tq), axis=0)
            m8[hh] = cm if m8[hh] is None else jnp.maximum(m8[hh], cm)
    outs = []
    for hh in range(2):
        m_ref[hh] = jnp.max(m8[hh], axis=0, keepdims=True)
        o = acc[hh][:V_HEAD_DIM] / acc[hh][V_HEAD_DIM:V_HEAD_DIM + 1]
        ssq = jnp.sum(o * o, axis=0, keepdims=True)
        outs.append(o * jax.lax.rsqrt(ssq * (1.0 / V_HEAD_DIM) + EPS))
    o_pair = jnp.concatenate(outs, axis=0).T
    o_ref[0] = (o_pair * g_ref[...]).astype(o_ref.dtype)


def _attention(layer, q, k, v, g_attn):
    b, nh, s, _ = q.shape
    n_pairs = nh // 2
    nq = s // TQ
    n_tiles = b * n_pairs * nq

    def tile_index(t):
        return t // (n_pairs * nq), (t // nq) % n_pairs, t % nq

    def scored(t):
        return tile_index(jnp.minimum(t, n_tiles - 1))

    def finished(t):
        return tile_index(jnp.maximum(t - 1, 0))

    def q_map(t):
        bi, j, qi = scored(t)
        return bi, j, qi, 0

    def k_map(t):
        bi, j, _ = scored(t)
        return bi, j, 0, 0

    def v_map(t):
        bi, j, _ = finished(t)
        return bi, j, 0, 0

    def g_map(t):
        return layer, 0, finished(t)[1]

    def o_map(t):
        bi, j, qi = finished(t)
        return bi, qi, j

    return pl.pallas_call(
        _attention_kernel,
        grid=(n_tiles + 1,),
        in_specs=[
            pl.BlockSpec((1, 2, TQ, HEAD_PAD), q_map),
            pl.BlockSpec((1, 2, s, HEAD_PAD), k_map),
            pl.BlockSpec((1, 1, LANES, s), v_map),
            pl.BlockSpec((None, 1, LANES), g_map),
        ],
        out_specs=pl.BlockSpec((1, TQ, LANES), o_map),
        out_shape=jax.ShapeDtypeStruct((b, s, ATTN_WIDTH), jnp.bfloat16),
        scratch_shapes=[pltpu.VMEM((2, s, TQ), jnp.float32), pltpu.VMEM((2, 1, TQ), jnp.float32)],
        compiler_params=pltpu.CompilerParams(
            dimension_semantics=("arbitrary",), vmem_limit_bytes=VMEM_LIMIT),
        name="attention",
    )(q, k, v, g_attn)


def _tail_kernel(x_ref, an_ref, cn_ref, p_ref, wout_ref, gpost_ref, gpmlp_ref, wup_ref, wdn_ref,
                 gomlp_ref, gpple_ref, wg_ref, wp_ref, gople_ref, o_ref):
    mix = _dot(an_ref[0], wout_ref[:ATTN_WIDTH, :]) + _dot(cn_ref[0], wout_ref[ATTN_WIDTH:, :])
    x = x_ref[0] + _rms(mix, gpost_ref[...])

    hm = _rms(x, gpmlp_ref[...]).astype(jnp.bfloat16)
    f = jnp.zeros(x.shape, jnp.float32)
    for j in range(D_FF // FF_CHUNK):
        sl = slice(j * FF_CHUNK, (j + 1) * FF_CHUNK)
        a = jnp.maximum(_dot(hm, wup_ref[:, sl]), 0.0)
        f = f + _dot((a * a).astype(jnp.bfloat16), wdn_ref[sl, :])
    x = x + _rms(f, gomlp_ref[...])

    hg = _rms(x, gpple_ref[...]).astype(jnp.bfloat16)
    gate = jax.nn.sigmoid(_dot(hg, wg_ref[...]))
    e = _dot(p_ref[0].astype(jnp.bfloat16), wp_ref[...])
    o_ref[0] = x + _rms(gate * e, gople_ref[...])


def _tail(layer, x, an, cn, p, wout, gpost, gpmlp, wup, wdn, gomlp, gpple, wg, wp, gople):
    b, s, d = x.shape
    tm = TM_TAIL
    tile = lambda w: pl.BlockSpec((1, tm, w), lambda bi, ti: (bi, ti, 0))
    resident = lambda a: pl.BlockSpec((None,) + a.shape[1:], lambda bi, ti: (layer, 0, 0),
                                      pipeline_mode=pl.Buffered(1))
    return pl.pallas_call(
        _tail_kernel,
        grid=(b, s // tm),
        in_specs=[tile(d), tile(ATTN_WIDTH), tile(CONV_CHANNELS),
                  pl.BlockSpec((None, 1, tm, PLE_DIM), lambda bi, ti: (layer, bi, ti, 0)),
                  resident(wout), resident(gpost), resident(gpmlp), resident(wup), resident(wdn),
                  resident(gomlp), resident(gpple), resident(wg), resident(wp), resident(gople)],
        out_specs=tile(d),
        out_shape=jax.ShapeDtypeStruct(x.shape, x.dtype),
        compiler_params=pltpu.CompilerParams(
            dimension_semantics=("arbitrary", "arbitrary"), vmem_limit_bytes=VMEM_LIMIT),
        name="tail",
    )(x, an, cn, p, wout, gpost, gpmlp, wup, wdn, gomlp, gpple, wg, wp, gople)


def _rope_tables(seq):
    inv_freq = 1.0 / (ROPE_THETA ** (jnp.arange(0, QK_ROPE_DIM, 2, dtype=jnp.float32) / QK_ROPE_DIM))
    ang = jnp.arange(seq, dtype=jnp.float32)[:, None] * inv_freq[None, :]
    cos, sin = jnp.cos(ang), jnp.sin(ang)
    reps = LANES // QK_ROPE_DIM
    cos_rep = jnp.tile(jnp.concatenate([cos, cos], axis=1), (1, reps))
    sin_rep = jnp.tile(jnp.concatenate([-sin, sin], axis=1), (1, reps))
    return jnp.concatenate([QK_SCALE * cos_rep, QK_SCALE * sin_rep, cos_rep, sin_rep], axis=1)


def _head_block(nope, rope, layout):
    z32 = jnp.zeros(nope.shape[:-1] + (QK_ROPE_DIM,), nope.dtype)
    rope = z32 if rope is None else rope
    n = QK_ROPE_DIM
    pieces = {0: [rope, nope, z32],
              1: [nope[..., :n], rope, nope[..., n:], z32],
              2: [nope, rope, z32],
              3: [nope, z32, rope]}[layout]
    return jnp.concatenate(pieces, axis=-1)


def _prep_w_in(w_in):
    o_kr = Q_LORA_RANK + KV_LORA_RANK
    kr = w_in[:, :, o_kr:o_kr + QK_ROPE_DIM]
    ext = jnp.concatenate([w_in[:, :, :o_kr]] + [kr] * (HEAD_PAD // QK_ROPE_DIM)
                          + [w_in[:, :, o_kr + QK_ROPE_DIM:]], axis=-1)
    return ext.astype(jnp.bfloat16)


def _prep_w_q(w_q_b):
    depth, r, _ = w_q_b.shape
    half = QK_ROPE_DIM // 2
    w = w_q_b.reshape(depth, r, N_HEADS, QK_HEAD_DIM)
    blocks, companions = [], []
    for hd in range(N_HEADS):
        nope, rope = w[:, :, hd, :QK_NOPE_DIM], w[:, :, hd, QK_NOPE_DIM:]
        blocks.append(_head_block(nope, rope, hd % N_ROPE_LAYOUTS))
        companions += [rope[..., half:], rope[..., :half]]
    return jnp.concatenate(blocks + companions, axis=-1).astype(jnp.bfloat16)


def _prep_w_kv(w_kv_b):
    depth, r, _ = w_kv_b.shape
    w = w_kv_b.reshape(depth, r, N_HEADS, QK_NOPE_DIM + V_HEAD_DIM)
    blocks = [_head_block(w[:, :, hd, :QK_NOPE_DIM], None, hd % N_ROPE_LAYOUTS) for hd in range(N_HEADS)]
    values = [w[:, :, hd, QK_NOPE_DIM:] for hd in range(N_HEADS)]
    return jnp.concatenate(blocks + values, axis=-1).astype(jnp.bfloat16)


def kernel(x, p, g_pre_mix, w_in, g_q_a, w_q_b, g_kv_a, w_kv_b, w_conv, g_attn_out, g_conv_out, w_out, g_post_mix, g_pre_mlp, w_mlp_up, w_mlp_down, g_post_mlp, g_pre_ple, w_ple_gate, w_ple_proj, g_post_ple):
    depth = w_in.shape[0]
    s = x.shape[1]
    tab = _rope_tables(s)
    win = _prep_w_in(w_in)
    wq = _prep_w_q(w_q_b)
    wkv = _prep_w_kv(w_kv_b)
    bf = lambda w: w.astype(jnp.bfloat16)
    wout, wup, wdn, wg, wp = bf(w_out), bf(w_mlp_up), bf(w_mlp_down), bf(w_ple_gate), bf(w_ple_proj)
    rows = lambda g: g[:, None, :]
    gains = {name: rows(g) for name, g in dict(
        pre_mix=g_pre_mix, q_a=g_q_a, kv_a=g_kv_a, attn_out=g_attn_out, conv_out=g_conv_out,
        post_mix=g_post_mix, pre_mlp=g_pre_mlp, post_mlp=g_post_mlp, pre_ple=g_pre_ple,
        post_ple=g_post_ple).items()}

    for i in range(depth):
        q, k, v, cn = _in_proj(i, x, tab, gains["pre_mix"], win, gains["q_a"], wq, gains["kv_a"], wkv,
                               w_conv, gains["conv_out"])
        an = _attention(i, q, k, v, gains["attn_out"])
        x = _tail(i, x, an, cn, p, wout, gains["post_mix"], gains["pre_mlp"], wup, wdn,
                  gains["post_mlp"], gains["pre_ple"], wg, wp, gains["post_ple"])
    return x
```

```python
import functools
import math

import jax
import jax.numpy as jnp
from jax.experimental import pallas as pl
from jax.experimental.pallas import tpu as pltpu

D_MODEL = 1024
N_HEADS = 8
V_HEAD_DIM = 64
QK_NOPE_DIM = 64
QK_ROPE_DIM = 32
QK_HEAD_DIM = QK_NOPE_DIM + QK_ROPE_DIM
Q_LORA_RANK = 384
KV_LORA_RANK = 256
CONV_CHANNELS = 512
ATTN_WIDTH = 512
GROUP_WIDTH = 64
D_FF = 4096
PLE_DIM = 256
ROPE_THETA = 10000.0
EPS = 1e-6

LANES = 128
SUBLANES = 8
ONES_ROWS = 16
HEAD_PAD = LANES
N_ROPE_LAYOUTS = LANES // QK_ROPE_DIM
QK_SCALE = math.log2(math.e) / math.sqrt(QK_HEAD_DIM)
HALO_ROWS = 8

C_QA = 0
C_KVA = C_QA + Q_LORA_RANK
C_KR = C_KVA + KV_LORA_RANK
C_GB = C_KR + HEAD_PAD
C_GC = C_GB + CONV_CHANNELS
C_CI = C_GC + CONV_CHANNELS
IN_EXT = C_CI + CONV_CHANNELS

TM_IN = 512
IN_SUB = 256
TQ = 512
CK = 256
TM_TAIL = 512
TAIL_SUB = 256
FF_CHUNK = 1024
VMEM_LIMIT = 56 * 1024 * 1024


def _rms(x, g):
    return x * jax.lax.rsqrt(jnp.mean(x * x, axis=-1, keepdims=True) + EPS) * g


def _group_rms_128(c, g):
    lane = jax.lax.broadcasted_iota(jnp.int32, c.shape, 1)
    lo = lane < GROUP_WIDTH
    sq = c * c
    s0 = jnp.sum(jnp.where(lo, sq, 0.0), axis=-1, keepdims=True)
    s1 = jnp.sum(jnp.where(lo, 0.0, sq), axis=-1, keepdims=True)
    r0 = jax.lax.rsqrt(s0 * (1.0 / GROUP_WIDTH) + EPS)
    r1 = jax.lax.rsqrt(s1 * (1.0 / GROUP_WIDTH) + EPS)
    return c * jnp.where(lo, r0, r1) * g


def _dot(a, b):
    return jnp.dot(a, b, preferred_element_type=jnp.float32)


def _rope_window(layout):
    lane = jax.lax.broadcasted_iota(jnp.int32, (1, LANES), 1)
    lo = layout * QK_ROPE_DIM
    return (lane >= lo) & (lane < lo + QK_ROPE_DIM)


def _in_proj_kernel(x_ref, xp_ref, xn_ref, tab_ref, gpre_ref, win_ref, gqa_ref, wq_ref,
                    gkva_ref, wkv_ref, wconv_ref, gconv_ref,
                    q_ref, k_ref, v_ref, cn_ref):
    t = pl.program_id(1)
    nt = pl.num_programs(1)
    n_sub = x_ref.shape[1] // IN_SUB
    rows = [slice(i * IN_SUB, (i + 1) * IN_SUB) for i in range(n_sub)]
    st = [dict() for _ in range(n_sub)]
    gpre = gpre_ref[...]
    windows = [_rope_window(r) for r in range(N_ROPE_LAYOUTS)]

    xh = jnp.concatenate([xp_ref[0], xn_ref[0]], axis=0)
    zh = _dot(_rms(xh, gpre).astype(jnp.bfloat16), win_ref[:, C_GC:])
    uh = zh[:, :CONV_CHANNELS] * zh[:, CONV_CHANNELS:]
    u_first = jnp.where(t > 0, uh[HALO_ROWS - 1:HALO_ROWS], 0.0)
    u_last = jnp.where(t < nt - 1, uh[HALO_ROWS:HALO_ROWS + 1], 0.0)

    def norm(i):
        st[i]["h"] = _rms(x_ref[0, rows[i], :], gpre).astype(jnp.bfloat16)

    def project(i):
        h = st[i].pop("h")
        st[i]["z_att"] = _dot(h, win_ref[:, :C_GB])
        st[i]["z_cv"] = _dot(h, win_ref[:, C_GB:])

    def latent_norms(i):
        z_att, z_cv = st[i].pop("z_att"), st[i].pop("z_cv")
        st[i]["qa"] = _rms(z_att[:, C_QA:C_KVA], gqa_ref[...]).astype(jnp.bfloat16)
        st[i]["kva"] = _rms(z_att[:, C_KVA:C_KR], gkva_ref[...]).astype(jnp.bfloat16)
        tab = tab_ref[rows[i], :]
        zkr = z_att[:, C_KR:C_GB]
        st[i]["krr"] = (zkr * tab[:, 2 * LANES:3 * LANES]
                        + pltpu.roll(zkr, QK_ROPE_DIM // 2, 1) * tab[:, 3 * LANES:4 * LANES])
        st[i]["gb"] = z_cv[:, :CONV_CHANNELS]
        st[i]["u"] = z_cv[:, CONV_CHANNELS:2 * CONV_CHANNELS] * z_cv[:, 2 * CONV_CHANNELS:]

    def expand(i):
        st[i]["qq"] = _dot(st[i].pop("qa"), wq_ref[...])
        st[i]["kv"] = _dot(st[i].pop("kva"), wkv_ref[...])

    def finish(i):
        tab = tab_ref[rows[i], :]
        cos_q, sin_q = tab[:, 0:LANES], tab[:, LANES:2 * LANES]
        qq, kv, krr = st[i].pop("qq"), st[i].pop("kv"), st[i].pop("krr")
        q_cos = [jnp.where(w, cos_q, QK_SCALE) for w in windows]
        q_sin = [jnp.where(w, sin_q, 0.0) for w in windows]
        kr_win = [jnp.where(w, krr, 0.0) for w in windows]
        for hd in range(N_HEADS):
            r = hd % N_ROPE_LAYOUTS
            main = qq[:, hd * HEAD_PAD:(hd + 1) * HEAD_PAD]
            lo = N_HEADS * HEAD_PAD + (hd // N_ROPE_LAYOUTS) * LANES
            q_ref[0, hd, rows[i], :] = (main * q_cos[r] + qq[:, lo:lo + LANES] * q_sin[r]).astype(q_ref.dtype)
            k_ref[0, hd, rows[i], :] = (kv[:, hd * HEAD_PAD:(hd + 1) * HEAD_PAD] + kr_win[r]).astype(k_ref.dtype)
        for j in range(N_HEADS // 2):
            lo = N_HEADS * HEAD_PAD + j * LANES
            v_ref[0, j, :, rows[i]] = kv[:, lo:lo + LANES].T.astype(v_ref.dtype)

        u = st[i]["u"]
        u_prev = u_first if i == 0 else st[i - 1]["u"][IN_SUB - 1:IN_SUB]
        u_next = u_last if i == n_sub - 1 else st[i + 1]["u"][0:1]
        r_idx = jax.lax.broadcasted_iota(jnp.int32, u.shape, 0)
        u_dn = jnp.where(r_idx == 0, u_prev, pltpu.roll(u, 1, 0))
        u_up = jnp.where(r_idx == IN_SUB - 1, u_next, pltpu.roll(u, IN_SUB - 1, 0))
        wc = wconv_ref[...]
        cv = st[i].pop("gb") * (u_dn * wc[0:1] + u * wc[1:2] + u_up * wc[2:3])
        gconv = gconv_ref[...]
        for j in range(CONV_CHANNELS // LANES):
            sl = slice(j * LANES, (j + 1) * LANES)
            cn_ref[0, rows[i], sl] = _group_rms_128(cv[:, sl], gconv[:, sl]).astype(cn_ref.dtype)

    stages = [norm, project, latent_norms, expand, finish]
    for step in range(len(stages) + n_sub - 1):
        for i in range(n_sub):
            if 0 <= step - i < len(stages):
                stages[step - i](i)


def _layer_spec(a, layer):
    return pl.BlockSpec((None,) + a.shape[1:], lambda *_: (layer, 0, 0))


def _in_proj(layer, x, tab, gpre, win, gqa, wq, gkva, wkv, wconv, gconv):
    b, s, d = x.shape
    tm = TM_IN
    nt = s // tm
    hb = tm // HALO_ROWS
    n_halo = s // HALO_ROWS
    return pl.pallas_call(
        _in_proj_kernel,
        grid=(b, nt),
        in_specs=[
            pl.BlockSpec((1, tm, d), lambda bi, ti: (bi, ti, 0)),
            pl.BlockSpec((1, HALO_ROWS, d), lambda bi, ti: (bi, jnp.maximum(ti * hb - 1, 0), 0)),
            pl.BlockSpec((1, HALO_ROWS, d), lambda bi, ti: (bi, jnp.minimum((ti + 1) * hb, n_halo - 1), 0)),
            pl.BlockSpec((tm, tab.shape[1]), lambda bi, ti: (ti, 0)),
        ] + [_layer_spec(a, layer) for a in (gpre, win, gqa, wq, gkva, wkv, wconv, gconv)],
        out_specs=[
            pl.BlockSpec((1, N_HEADS, tm, HEAD_PAD), lambda bi, ti: (bi, 0, ti, 0)),
            pl.BlockSpec((1, N_HEADS, tm, HEAD_PAD), lambda bi, ti: (bi, 0, ti, 0)),
            pl.BlockSpec((1, N_HEADS // 2, LANES, tm), lambda bi, ti: (bi, 0, 0, ti)),
            pl.BlockSpec((1, tm, CONV_CHANNELS), lambda bi, ti: (bi, ti, 0)),
        ],
        out_shape=[
            jax.ShapeDtypeStruct((b, N_HEADS, s, HEAD_PAD), jnp.bfloat16),
            jax.ShapeDtypeStruct((b, N_HEADS, s, HEAD_PAD), jnp.bfloat16),
            jax.ShapeDtypeStruct((b, N_HEADS // 2, LANES, s), jnp.bfloat16),
            jax.ShapeDtypeStruct((b, s, CONV_CHANNELS), jnp.bfloat16),
        ],
        compiler_params=pltpu.CompilerParams(
            dimension_semantics=("arbitrary", "arbitrary"), vmem_limit_bytes=VMEM_LIMIT),
        name="in_proj",
    )(x, x, x, tab, gpre, win, gqa, wq, gkva, wkv, wconv, gconv)


def _attention_kernel(q_ref, k_ref, v_ref, g_ref, o_ref, s_ref, m_ref):
    tq = q_ref.shape[2]
    s_len = k_ref.shape[2]
    n_chunks = s_len // CK

    @pl.when(pl.program_id(0) == 0)
    def _():
        s_ref[...] = jnp.zeros(s_ref.shape, s_ref.dtype)
        m_ref[...] = jnp.zeros(m_ref.shape, m_ref.dtype)

    ones = jnp.ones((ONES_ROWS, CK), jnp.bfloat16)
    m_prev = [m_ref[hh] for hh in range(2)]
    q = [q_ref[0, hh] for hh in range(2)]
    acc = [jnp.zeros((V_HEAD_DIM + ONES_ROWS, tq), jnp.float32) for _ in range(2)]
    m8 = [None, None]
    for c in range(n_chunks):
        rows = slice(c * CK, (c + 1) * CK)
        for hh in range(2):
            pt = jnp.exp2(s_ref[hh, rows, :] - m_prev[hh]).astype(jnp.bfloat16)
            vt = v_ref[0, 0, hh * V_HEAD_DIM:(hh + 1) * V_HEAD_DIM, rows]
            acc[hh] = acc[hh] + _dot(jnp.concatenate([vt, ones], axis=0), pt)
        for hh in range(2):
            st = jax.lax.dot_general(k_ref[0, hh, rows, :], q[hh], (((1,), (1,)), ((), ())),
                                     preferred_element_type=jnp.float32)
            s_ref[hh, rows, :] = st
            cm = jnp.max(st.reshape(CK // SUBLANES, SUBLANES, tq), axis=0)
            m8[hh] = cm if m8[hh] is None else jnp.maximum(m8[hh], cm)
    outs = []
    for hh in range(2):
        m_ref[hh] = jnp.max(m8[hh], axis=0, keepdims=True)
        o = acc[hh][:V_HEAD_DIM] / acc[hh][V_HEAD_DIM:V_HEAD_DIM + 1]
        ssq = jnp.sum(o * o, axis=0, keepdims=True)
        outs.append(o * jax.lax.rsqrt(ssq * (1.0 / V_HEAD_DIM) + EPS))
    o_pair = jnp.concatenate(outs, axis=0).T
    o_ref[0] = (o_pair * g_ref[...]).astype(o_ref.dtype)


def _attention(layer, q, k, v, g_attn):
    b, nh, s, _ = q.shape
    n_pairs = nh // 2
    nq = s // TQ
    n_tiles = b * n_pairs * nq

    def tile_index(t):
        return t // (n_pairs * nq), (t // nq) % n_pairs, t % nq

    def scored(t):
        return tile_index(jnp.minimum(t, n_tiles - 1))

    def finished(t):
        return tile_index(jnp.maximum(t - 1, 0))

    def q_map(t):
        bi, j, qi = scored(t)
        return bi, j, qi, 0

    def k_map(t):
        bi, j, _ = scored(t)
        return bi, j, 0, 0

    def v_map(t):
        bi, j, _ = finished(t)
        return bi, j, 0, 0

    def g_map(t):
        return layer, 0, finished(t)[1]

    def o_map(t):
        bi, j, qi = finished(t)
        return bi, qi, j

    return pl.pallas_call(
        _attention_kernel,
        grid=(n_tiles + 1,),
        in_specs=[
            pl.BlockSpec((1, 2, TQ, HEAD_PAD), q_map),
            pl.BlockSpec((1, 2, s, HEAD_PAD), k_map),
            pl.BlockSpec((1, 1, LANES, s), v_map),
            pl.BlockSpec((None, 1, LANES), g_map),
        ],
        out_specs=pl.BlockSpec((1, TQ, LANES), o_map),
        out_shape=jax.ShapeDtypeStruct((b, s, ATTN_WIDTH), jnp.bfloat16),
        scratch_shapes=[pltpu.VMEM((2, s, TQ), jnp.float32), pltpu.VMEM((2, 1, TQ), jnp.float32)],
        compiler_params=pltpu.CompilerParams(
            dimension_semantics=("arbitrary",), vmem_limit_bytes=VMEM_LIMIT),
        name="attention",
    )(q, k, v, g_attn)


def _tail_kernel(x_ref, an_ref, cn_ref, p_ref, wout_ref, gpost_ref, gpmlp_ref, wup_ref, wdn_ref,
                 gomlp_ref, gpple_ref, wg_ref, wp_ref, gople_ref, o_ref):
    n_sub = x_ref.shape[1] // TAIL_SUB
    rows = [slice(i * TAIL_SUB, (i + 1) * TAIL_SUB) for i in range(n_sub)]
    st = [dict() for _ in range(n_sub)]

    def out_proj(i):
        st[i]["mix"] = (_dot(an_ref[0, rows[i], :], wout_ref[:ATTN_WIDTH, :])
                        + _dot(cn_ref[0, rows[i], :], wout_ref[ATTN_WIDTH:, :]))

    def post_mix(i):
        x = x_ref[0, rows[i], :] + _rms(st[i].pop("mix"), gpost_ref[...])
        st[i]["x"] = x
        st[i]["h"] = _rms(x, gpmlp_ref[...]).astype(jnp.bfloat16)

    def mlp(i):
        hm = st[i].pop("h")
        f = None
        for j in range(D_FF // FF_CHUNK):
            sl = slice(j * FF_CHUNK, (j + 1) * FF_CHUNK)
            a = jnp.maximum(_dot(hm, wup_ref[:, sl]), 0.0)
            d = _dot((a * a).astype(jnp.bfloat16), wdn_ref[sl, :])
            f = d if f is None else f + d
        st[i]["f"] = f

    def post_mlp(i):
        x = st[i]["x"] + _rms(st[i].pop("f"), gomlp_ref[...])
        st[i]["x"] = x
        st[i]["h"] = _rms(x, gpple_ref[...]).astype(jnp.bfloat16)

    def ple_proj(i):
        st[i]["g"] = _dot(st[i].pop("h"), wg_ref[...])
        st[i]["e"] = _dot(p_ref[0, rows[i], :].astype(jnp.bfloat16), wp_ref[...])

    def post_ple(i):
        gate = jax.nn.sigmoid(st[i].pop("g"))
        o_ref[0, rows[i], :] = st[i].pop("x") + _rms(gate * st[i].pop("e"), gople_ref[...])

    stages = [out_proj, post_mix, mlp, post_mlp, ple_proj, post_ple]
    for step in range(len(stages) + n_sub - 1):
        for i in range(n_sub):
            if 0 <= step - i < len(stages):
                stages[step - i](i)


def _tail(layer, x, an, cn, p, wout, gpost, gpmlp, wup, wdn, gomlp, gpple, wg, wp, gople):
    b, s, d = x.shape
    tm = TM_TAIL
    tile = lambda w: pl.BlockSpec((1, tm, w), lambda bi, ti: (bi, ti, 0))
    resident = lambda a: pl.BlockSpec((None,) + a.shape[1:], lambda bi, ti: (layer, 0, 0),
                                      pipeline_mode=pl.Buffered(1))
    return pl.pallas_call(
        _tail_kernel,
        grid=(b, s // tm),
        in_specs=[tile(d), tile(ATTN_WIDTH), tile(CONV_CHANNELS),
                  pl.BlockSpec((None, 1, tm, PLE_DIM), lambda bi, ti: (layer, bi, ti, 0)),
                  resident(wout), resident(gpost), resident(gpmlp), resident(wup), resident(wdn),
                  resident(gomlp), resident(gpple), resident(wg), resident(wp), resident(gople)],
        out_specs=tile(d),
        out_shape=jax.ShapeDtypeStruct(x.shape, x.dtype),
        compiler_params=pltpu.CompilerParams(
            dimension_semantics=("arbitrary", "arbitrary"), vmem_limit_bytes=VMEM_LIMIT),
        name="tail",
    )(x, an, cn, p, wout, gpost, gpmlp, wup, wdn, gomlp, gpple, wg, wp, gople)


def _rope_tables(seq):
    inv_freq = 1.0 / (ROPE_THETA ** (jnp.arange(0, QK_ROPE_DIM, 2, dtype=jnp.float32) / QK_ROPE_DIM))
    ang = jnp.arange(seq, dtype=jnp.float32)[:, None] * inv_freq[None, :]
    cos, sin = jnp.cos(ang), jnp.sin(ang)
    reps = LANES // QK_ROPE_DIM
    cos_rep = jnp.tile(jnp.concatenate([cos, cos], axis=1), (1, reps))
    sin_rep = jnp.tile(jnp.concatenate([-sin, sin], axis=1), (1, reps))
    return jnp.concatenate([QK_SCALE * cos_rep, QK_SCALE * sin_rep, cos_rep, sin_rep], axis=1)


def _head_block(nope, rope, layout):
    z32 = jnp.zeros(nope.shape[:-1] + (QK_ROPE_DIM,), nope.dtype)
    rope = z32 if rope is None else rope
    n = QK_ROPE_DIM
    pieces = {0: [rope, nope, z32],
              1: [nope[..., :n], rope, nope[..., n:], z32],
              2: [nope, rope, z32],
              3: [nope, z32, rope]}[layout]
    return jnp.concatenate(pieces, axis=-1)


def _prep_w_in(w_in):
    o_kr = Q_LORA_RANK + KV_LORA_RANK
    kr = w_in[:, :, o_kr:o_kr + QK_ROPE_DIM]
    ext = jnp.concatenate([w_in[:, :, :o_kr]] + [kr] * (HEAD_PAD // QK_ROPE_DIM)
                          + [w_in[:, :, o_kr + QK_ROPE_DIM:]], axis=-1)
    return ext.astype(jnp.bfloat16)


def _prep_w_q(w_q_b):
    depth, r, _ = w_q_b.shape
    half = QK_ROPE_DIM // 2
    w = w_q_b.reshape(depth, r, N_HEADS, QK_HEAD_DIM)
    blocks, companions = [], []
    for hd in range(N_HEADS):
        nope, rope = w[:, :, hd, :QK_NOPE_DIM], w[:, :, hd, QK_NOPE_DIM:]
        blocks.append(_head_block(nope, rope, hd % N_ROPE_LAYOUTS))
        companions += [rope[..., half:], rope[..., :half]]
    return jnp.concatenate(blocks + companions, axis=-1).astype(jnp.bfloat16)


def _prep_w_kv(w_kv_b):
    depth, r, _ = w_kv_b.shape
    w = w_kv_b.reshape(depth, r, N_HEADS, QK_NOPE_DIM + V_HEAD_DIM)
    blocks = [_head_block(w[:, :, hd, :QK_NOPE_DIM], None, hd % N_ROPE_LAYOUTS) for hd in range(N_HEADS)]
    values = [w[:, :, hd, QK_NOPE_DIM:] for hd in range(N_HEADS)]
    return jnp.concatenate(blocks + values, axis=-1).astype(jnp.bfloat16)


def kernel(x, p, g_pre_mix, w_in, g_q_a, w_q_b, g_kv_a, w_kv_b, w_conv, g_attn_out, g_conv_out, w_out, g_post_mix, g_pre_mlp, w_mlp_up, w_mlp_down, g_post_mlp, g_pre_ple, w_ple_gate, w_ple_proj, g_post_ple):
    depth = w_in.shape[0]
    s = x.shape[1]
    tab = _rope_tables(s)
    win = _prep_w_in(w_in)
    wq = _prep_w_q(w_q_b)
    wkv = _prep_w_kv(w_kv_b)
    bf = lambda w: w.astype(jnp.bfloat16)
    wout, wup, wdn, wg, wp = bf(w_out), bf(w_mlp_up), bf(w_mlp_down), bf(w_ple_gate), bf(w_ple_proj)
    rows = lambda g: g[:, None, :]
    gains = {name: rows(g) for name, g in dict(
        pre_mix=g_pre_mix, q_a=g_q_a, kv_a=g_kv_a, attn_out=g_attn_out, conv_out=g_conv_out,
        post_mix=g_post_mix, pre_mlp=g_pre_mlp, post_mlp=g_post_mlp, pre_ple=g_pre_ple,
        post_ple=g_post_ple).items()}

    for i in range(depth):
        q, k, v, cn = _in_proj(i, x, tab, gains["pre_mix"], win, gains["q_a"], wq, gains["kv_a"], wkv,
                               w_conv, gains["conv_out"])
        an = _attention(i, q, k, v, gains["attn_out"])
        x = _tail(i, x, an, cn, p, wout, gains["post_mix"], gains["pre_mlp"], wup, wdn,
                  gains["post_mlp"], gains["pre_ple"], wg, wp, gains["post_ple"])
    return x
```

```python
import functools
import math

import jax
import jax.numpy as jnp
from jax.experimental import pallas as pl
from jax.experimental.pallas import tpu as pltpu

D_MODEL = 1024
N_HEADS = 8
V_HEAD_DIM = 64
QK_NOPE_DIM = 64
QK_ROPE_DIM = 32
QK_HEAD_DIM = QK_NOPE_DIM + QK_ROPE_DIM
Q_LORA_RANK = 384
KV_LORA_RANK = 256
CONV_CHANNELS = 512
ATTN_WIDTH = 512
GROUP_WIDTH = 64
D_FF = 4096
PLE_DIM = 256
ROPE_THETA = 10000.0
EPS = 1e-6

LANES = 128
SUBLANES = 8
ONES_ROWS = 16
HEAD_PAD = LANES
N_ROPE_LAYOUTS = LANES // QK_ROPE_DIM
QK_SCALE = math.log2(math.e) / math.sqrt(QK_HEAD_DIM)
HALO_ROWS = 8

C_QA = 0
C_KVA = C_QA + Q_LORA_RANK
C_KR = C_KVA + KV_LORA_RANK
C_GB = C_KR + HEAD_PAD
C_GC = C_GB + CONV_CHANNELS
C_CI = C_GC + CONV_CHANNELS
IN_EXT = C_CI + CONV_CHANNELS

TM_IN = 512
IN_SUB = 512
TQ = 512
CK = 256
TM_TAIL = 512
TAIL_SUB = 256
FF_CHUNK = 1024
VMEM_LIMIT = 56 * 1024 * 1024


def _rms(x, g):
    return x * jax.lax.rsqrt(jnp.mean(x * x, axis=-1, keepdims=True) + EPS) * g


def _group_rms_128(c, g):
    lane = jax.lax.broadcasted_iota(jnp.int32, c.shape, 1)
    lo = lane < GROUP_WIDTH
    sq = c * c
    s0 = jnp.sum(jnp.where(lo, sq, 0.0), axis=-1, keepdims=True)
    s1 = jnp.sum(jnp.where(lo, 0.0, sq), axis=-1, keepdims=True)
    r0 = jax.lax.rsqrt(s0 * (1.0 / GROUP_WIDTH) + EPS)
    r1 = jax.lax.rsqrt(s1 * (1.0 / GROUP_WIDTH) + EPS)
    return c * jnp.where(lo, r0, r1) * g


def _dot(a, b):
    return jnp.dot(a, b, preferred_element_type=jnp.float32)


def _rope_window(layout):
    lane = jax.lax.broadcasted_iota(jnp.int32, (1, LANES), 1)
    lo = layout * QK_ROPE_DIM
    return (lane >= lo) & (lane < lo + QK_ROPE_DIM)


def _in_proj_kernel(x_ref, xp_ref, xn_ref, tab_ref, gpre_ref, win_ref, gqa_ref, wq_ref,
                    gkva_ref, wkv_ref, wconv_ref, gconv_ref,
                    q_ref, k_ref, v_ref, cn_ref):
    t = pl.program_id(1)
    nt = pl.num_programs(1)
    n_sub = x_ref.shape[1] // IN_SUB
    rows = [slice(i * IN_SUB, (i + 1) * IN_SUB) for i in range(n_sub)]
    st = [dict() for _ in range(n_sub)]
    gpre = gpre_ref[...]
    windows = [_rope_window(r) for r in range(N_ROPE_LAYOUTS)]

    xh = jnp.concatenate([xp_ref[0], xn_ref[0]], axis=0)
    zh = _dot(_rms(xh, gpre).astype(jnp.bfloat16), win_ref[:, C_GC:])
    uh = zh[:, :CONV_CHANNELS] * zh[:, CONV_CHANNELS:]
    u_first = jnp.where(t > 0, uh[HALO_ROWS - 1:HALO_ROWS], 0.0)
    u_last = jnp.where(t < nt - 1, uh[HALO_ROWS:HALO_ROWS + 1], 0.0)

    def norm(i):
        st[i]["h"] = _rms(x_ref[0, rows[i], :], gpre).astype(jnp.bfloat16)

    def project(i):
        h = st[i].pop("h")
        st[i]["z_att"] = _dot(h, win_ref[:, :C_GB])
        st[i]["z_cv"] = _dot(h, win_ref[:, C_GB:])

    def latent_norms(i):
        z_att, z_cv = st[i].pop("z_att"), st[i].pop("z_cv")
        st[i]["qa"] = _rms(z_att[:, C_QA:C_KVA], gqa_ref[...]).astype(jnp.bfloat16)
        st[i]["kva"] = _rms(z_att[:, C_KVA:C_KR], gkva_ref[...]).astype(jnp.bfloat16)
        tab = tab_ref[rows[i], :]
        zkr = z_att[:, C_KR:C_GB]
        st[i]["krr"] = (zkr * tab[:, 2 * LANES:3 * LANES]
                        + pltpu.roll(zkr, QK_ROPE_DIM // 2, 1) * tab[:, 3 * LANES:4 * LANES])
        st[i]["gb"] = z_cv[:, :CONV_CHANNELS]
        st[i]["u"] = z_cv[:, CONV_CHANNELS:2 * CONV_CHANNELS] * z_cv[:, 2 * CONV_CHANNELS:]

    def expand(i):
        st[i]["qq"] = _dot(st[i].pop("qa"), wq_ref[...])
        st[i]["kv"] = _dot(st[i].pop("kva"), wkv_ref[...])

    def finish(i):
        tab = tab_ref[rows[i], :]
        cos_q, sin_q = tab[:, 0:LANES], tab[:, LANES:2 * LANES]
        qq, kv, krr = st[i].pop("qq"), st[i].pop("kv"), st[i].pop("krr")
        q_cos = [jnp.where(w, cos_q, QK_SCALE) for w in windows]
        q_sin = [jnp.where(w, sin_q, 0.0) for w in windows]
        kr_win = [jnp.where(w, krr, 0.0) for w in windows]
        for hd in range(N_HEADS):
            r = hd % N_ROPE_LAYOUTS
            main = qq[:, hd * HEAD_PAD:(hd + 1) * HEAD_PAD]
            lo = N_HEADS * HEAD_PAD + (hd // N_ROPE_LAYOUTS) * LANES
            qh = main * q_cos[r] + qq[:, lo:lo + LANES] * q_sin[r]
            q_ref[0, hd, :, rows[i]] = qh.T.astype(q_ref.dtype)
            k_ref[0, hd, rows[i], :] = (kv[:, hd * HEAD_PAD:(hd + 1) * HEAD_PAD] + kr_win[r]).astype(k_ref.dtype)
        for j in range(N_HEADS // 2):
            lo = N_HEADS * HEAD_PAD + j * LANES
            v_ref[0, j, :, rows[i]] = kv[:, lo:lo + LANES].T.astype(v_ref.dtype)

        u = st[i]["u"]
        u_prev = u_first if i == 0 else st[i - 1]["u"][IN_SUB - 1:IN_SUB]
        u_next = u_last if i == n_sub - 1 else st[i + 1]["u"][0:1]
        r_idx = jax.lax.broadcasted_iota(jnp.int32, u.shape, 0)
        u_dn = jnp.where(r_idx == 0, u_prev, pltpu.roll(u, 1, 0))
        u_up = jnp.where(r_idx == IN_SUB - 1, u_next, pltpu.roll(u, IN_SUB - 1, 0))
        wc = wconv_ref[...]
        cv = st[i].pop("gb") * (u_dn * wc[0:1] + u * wc[1:2] + u_up * wc[2:3])
        gconv = gconv_ref[...]
        for j in range(CONV_CHANNELS // LANES):
            sl = slice(j * LANES, (j + 1) * LANES)
            cn_ref[0, rows[i], sl] = _group_rms_128(cv[:, sl], gconv[:, sl]).astype(cn_ref.dtype)

    stages = [norm, project, latent_norms, expand, finish]
    for step in range(len(stages) + n_sub - 1):
        for i in range(n_sub):
            if 0 <= step - i < len(stages):
                stages[step - i](i)


def _layer_spec(a, layer):
    return pl.BlockSpec((None,) + a.shape[1:], lambda *_: (layer, 0, 0))


def _in_proj(layer, x, tab, gpre, win, gqa, wq, gkva, wkv, wconv, gconv):
    b, s, d = x.shape
    tm = TM_IN
    nt = s // tm
    hb = tm // HALO_ROWS
    n_halo = s // HALO_ROWS
    return pl.pallas_call(
        _in_proj_kernel,
        grid=(b, nt),
        in_specs=[
            pl.BlockSpec((1, tm, d), lambda bi, ti: (bi, ti, 0)),
            pl.BlockSpec((1, HALO_ROWS, d), lambda bi, ti: (bi, jnp.maximum(ti * hb - 1, 0), 0)),
            pl.BlockSpec((1, HALO_ROWS, d), lambda bi, ti: (bi, jnp.minimum((ti + 1) * hb, n_halo - 1), 0)),
            pl.BlockSpec((tm, tab.shape[1]), lambda bi, ti: (ti, 0)),
        ] + [_layer_spec(a, layer) for a in (gpre, win, gqa, wq, gkva, wkv, wconv, gconv)],
        out_specs=[
            pl.BlockSpec((1, N_HEADS, HEAD_PAD, tm), lambda bi, ti: (bi, 0, 0, ti)),
            pl.BlockSpec((1, N_HEADS, tm, HEAD_PAD), lambda bi, ti: (bi, 0, ti, 0)),
            pl.BlockSpec((1, N_HEADS // 2, LANES, tm), lambda bi, ti: (bi, 0, 0, ti)),
            pl.BlockSpec((1, tm, CONV_CHANNELS), lambda bi, ti: (bi, ti, 0)),
        ],
        out_shape=[
            jax.ShapeDtypeStruct((b, N_HEADS, HEAD_PAD, s), jnp.bfloat16),
            jax.ShapeDtypeStruct((b, N_HEADS, s, HEAD_PAD), jnp.bfloat16),
            jax.ShapeDtypeStruct((b, N_HEADS // 2, LANES, s), jnp.bfloat16),
            jax.ShapeDtypeStruct((b, s, CONV_CHANNELS), jnp.bfloat16),
        ],
        compiler_params=pltpu.CompilerParams(
            dimension_semantics=("arbitrary", "arbitrary"), vmem_limit_bytes=VMEM_LIMIT),
        name="in_proj",
    )(x, x, x, tab, gpre, win, gqa, wq, gkva, wkv, wconv, gconv)


def _attention_kernel(q_ref, k_ref, v_ref, g_ref, o_ref, s_ref, m_ref):
    tq = q_ref.shape[3]
    s_len = k_ref.shape[2]
    n_chunks = s_len // CK

    @pl.when(pl.program_id(0) == 0)
    def _():
        s_ref[...] = jnp.zeros(s_ref.shape, s_ref.dtype)
        m_ref[...] = jnp.zeros(m_ref.shape, m_ref.dtype)

    ones = jnp.ones((ONES_ROWS, CK), jnp.bfloat16)
    m_prev = [m_ref[hh] for hh in range(2)]
    qt = [q_ref[0, hh] for hh in range(2)]
    acc = [jnp.zeros((V_HEAD_DIM + ONES_ROWS, tq), jnp.float32) for _ in range(2)]
    m8 = [None, None]
    for c in range(n_chunks):
        rows = slice(c * CK, (c + 1) * CK)
        for hh in range(2):
            pt = jnp.exp2(s_ref[hh, rows, :] - m_prev[hh]).astype(jnp.bfloat16)
            vt = v_ref[0, 0, hh * V_HEAD_DIM:(hh + 1) * V_HEAD_DIM, rows]
            acc[hh] = acc[hh] + _dot(jnp.concatenate([vt, ones], axis=0), pt)
        for hh in range(2):
            st = _dot(k_ref[0, hh, rows, :], qt[hh])
            s_ref[hh, rows, :] = st
            cm = jnp.max(st.reshape(CK // SUBLANES, SUBLANES, tq), axis=0)
            m8[hh] = cm if m8[hh] is None else jnp.maximum(m8[hh], cm)
    outs = []
    for hh in range(2):
        m_ref[hh] = jnp.max(m8[hh], axis=0, keepdims=True)
        o = acc[hh][:V_HEAD_DIM] / acc[hh][V_HEAD_DIM:V_HEAD_DIM + 1]
        ssq = jnp.sum(o * o, axis=0, keepdims=True)
        outs.append(o * jax.lax.rsqrt(ssq * (1.0 / V_HEAD_DIM) + EPS))
    o_pair = jnp.concatenate(outs, axis=0).T
    o_ref[0] = (o_pair * g_ref[...]).astype(o_ref.dtype)


def _attention(layer, q, k, v, g_attn):
    b, nh, s, _ = k.shape
    n_pairs = nh // 2
    nq = s // TQ
    n_tiles = b * n_pairs * nq

    def tile_index(t):
        return t // (n_pairs * nq), (t // nq) % n_pairs, t % nq

    def scored(t):
        return tile_index(jnp.minimum(t, n_tiles - 1))

    def finished(t):
        return tile_index(jnp.maximum(t - 1, 0))

    def q_map(t):
        bi, j, qi = scored(t)
        return bi, j, 0, qi

    def k_map(t):
        bi, j, _ = scored(t)
        return bi, j, 0, 0

    def v_map(t):
        bi, j, _ = finished(t)
        return bi, j, 0, 0

    def g_map(t):
        return layer, 0, finished(t)[1]

    def o_map(t):
        bi, j, qi = finished(t)
        return bi, qi, j

    return pl.pallas_call(
        _attention_kernel,
        grid=(n_tiles + 1,),
        in_specs=[
            pl.BlockSpec((1, 2, HEAD_PAD, TQ), q_map),
            pl.BlockSpec((1, 2, s, HEAD_PAD), k_map),
            pl.BlockSpec((1, 1, LANES, s), v_map),
            pl.BlockSpec((None, 1, LANES), g_map),
        ],
        out_specs=pl.BlockSpec((1, TQ, LANES), o_map),
        out_shape=jax.ShapeDtypeStruct((b, s, ATTN_WIDTH), jnp.bfloat16),
        scratch_shapes=[pltpu.VMEM((2, s, TQ), jnp.float32), pltpu.VMEM((2, 1, TQ), jnp.float32)],
        compiler_params=pltpu.CompilerParams(
            dimension_semantics=("arbitrary",), vmem_limit_bytes=VMEM_LIMIT),
        name="attention",
    )(q, k, v, g_attn)


def _tail_kernel(x_ref, an_ref, cn_ref, p_ref, wout_ref, gpost_ref, gpmlp_ref, wup_ref, wdn_ref,
                 gomlp_ref, gpple_ref, wg_ref, wp_ref, gople_ref, o_ref):
    n_sub = x_ref.shape[1] // TAIL_SUB
    rows = [slice(i * TAIL_SUB, (i + 1) * TAIL_SUB) for i in range(n_sub)]
    st = [dict() for _ in range(n_sub)]

    def out_proj(i):
        st[i]["mix"] = (_dot(an_ref[0, rows[i], :], wout_ref[:ATTN_WIDTH, :])
                        + _dot(cn_ref[0, rows[i], :], wout_ref[ATTN_WIDTH:, :]))

    def post_mix(i):
        x = x_ref[0, rows[i], :] + _rms(st[i].pop("mix"), gpost_ref[...])
        st[i]["x"] = x
        st[i]["h"] = _rms(x, gpmlp_ref[...]).astype(jnp.bfloat16)

    def mlp(i):
        hm = st[i].pop("h")
        f = None
        for j in range(D_FF // FF_CHUNK):
            sl = slice(j * FF_CHUNK, (j + 1) * FF_CHUNK)
            a = jnp.maximum(_dot(hm, wup_ref[:, sl]), 0.0)
            d = _dot((a * a).astype(jnp.bfloat16), wdn_ref[sl, :])
            f = d if f is None else f + d
        st[i]["f"] = f

    def post_mlp(i):
        x = st[i]["x"] + _rms(st[i].pop("f"), gomlp_ref[...])
        st[i]["x"] = x
        st[i]["h"] = _rms(x, gpple_ref[...]).astype(jnp.bfloat16)

    def ple_proj(i):
        st[i]["g"] = _dot(st[i].pop("h"), wg_ref[...])
        st[i]["e"] = _dot(p_ref[0, rows[i], :].astype(jnp.bfloat16), wp_ref[...])

    def post_ple(i):
        gate = jax.nn.sigmoid(st[i].pop("g"))
        o_ref[0, rows[i], :] = st[i].pop("x") + _rms(gate * st[i].pop("e"), gople_ref[...])

    stages = [out_proj, post_mix, mlp, post_mlp, ple_proj, post_ple]
    for step in range(len(stages) + n_sub - 1):
        for i in range(n_sub):
            if 0 <= step - i < len(stages):
                stages[step - i](i)


def _tail(layer, x, an, cn, p, wout, gpost, gpmlp, wup, wdn, gomlp, gpple, wg, wp, gople):
    b, s, d = x.shape
    tm = TM_TAIL
    tile = lambda w: pl.BlockSpec((1, tm, w), lambda bi, ti: (bi, ti, 0))
    resident = lambda a: pl.BlockSpec((None,) + a.shape[1:], lambda bi, ti: (layer, 0, 0),
                                      pipeline_mode=pl.Buffered(1))
    return pl.pallas_call(
        _tail_kernel,
        grid=(b, s // tm),
        in_specs=[tile(d), tile(ATTN_WIDTH), tile(CONV_CHANNELS),
                  pl.BlockSpec((None, 1, tm, PLE_DIM), lambda bi, ti: (layer, bi, ti, 0)),
                  resident(wout), resident(gpost), resident(gpmlp), resident(wup), resident(wdn),
                  resident(gomlp), resident(gpple), resident(wg), resident(wp), resident(gople)],
        out_specs=tile(d),
        out_shape=jax.ShapeDtypeStruct(x.shape, x.dtype),
        compiler_params=pltpu.CompilerParams(
            dimension_semantics=("arbitrary", "arbitrary"), vmem_limit_bytes=VMEM_LIMIT),
        name="tail",
    )(x, an, cn, p, wout, gpost, gpmlp, wup, wdn, gomlp, gpple, wg, wp, gople)


def _rope_tables(seq):
    inv_freq = 1.0 / (ROPE_THETA ** (jnp.arange(0, QK_ROPE_DIM, 2, dtype=jnp.float32) / QK_ROPE_DIM))
    ang = jnp.arange(seq, dtype=jnp.float32)[:, None] * inv_freq[None, :]
    cos, sin = jnp.cos(ang), jnp.sin(ang)
    reps = LANES // QK_ROPE_DIM
    cos_rep = jnp.tile(jnp.concatenate([cos, cos], axis=1), (1, reps))
    sin_rep = jnp.tile(jnp.concatenate([-sin, sin], axis=1), (1, reps))
    return jnp.concatenate([QK_SCALE * cos_rep, QK_SCALE * sin_rep, cos_rep, sin_rep], axis=1)


def _head_block(nope, rope, layout):
    z32 = jnp.zeros(nope.shape[:-1] + (QK_ROPE_DIM,), nope.dtype)
    rope = z32 if rope is None else rope
    n = QK_ROPE_DIM
    pieces = {0: [rope, nope, z32],
              1: [nope[..., :n], rope, nope[..., n:], z32],
              2: [nope, rope, z32],
              3: [nope, z32, rope]}[layout]
    return jnp.concatenate(pieces, axis=-1)


def _prep_w_in_kernel(w_ref, o_ref):
    x = w_ref[0]
    o_ref[0, :, :C_KR] = x[:, :C_KR].astype(o_ref.dtype)
    blk = x[:, C_KR:C_KR + LANES]
    lane = jax.lax.broadcasted_iota(jnp.int32, blk.shape, 1)
    kr = jnp.where(lane < QK_ROPE_DIM, blk, 0.0)
    rep = kr
    for i in range(1, LANES // QK_ROPE_DIM):
        rep = rep + pltpu.roll(kr, i * QK_ROPE_DIM, 1)
    o_ref[0, :, C_KR:C_GB] = rep.astype(o_ref.dtype)
    o_ref[0, :, C_GB:] = x[:, C_KR + QK_ROPE_DIM:].astype(o_ref.dtype)


def _prep_w_in(w_in):
    depth, d, n = w_in.shape
    rows = 256
    return pl.pallas_call(
        _prep_w_in_kernel,
        grid=(depth, d // rows),
        in_specs=[pl.BlockSpec((1, rows, n), lambda i, r: (i, r, 0))],
        out_specs=pl.BlockSpec((1, rows, IN_EXT), lambda i, r: (i, r, 0)),
        out_shape=jax.ShapeDtypeStruct((depth, d, IN_EXT), jnp.bfloat16),
        compiler_params=pltpu.CompilerParams(dimension_semantics=("arbitrary", "arbitrary")),
        name="prep_w_in",
    )(w_in)


def _prep_w_q(w_q_b):
    depth, r, _ = w_q_b.shape
    half = QK_ROPE_DIM // 2
    w = w_q_b.reshape(depth, r, N_HEADS, QK_HEAD_DIM)
    blocks, companions = [], []
    for hd in range(N_HEADS):
        nope, rope = w[:, :, hd, :QK_NOPE_DIM], w[:, :, hd, QK_NOPE_DIM:]
        blocks.append(_head_block(nope, rope, hd % N_ROPE_LAYOUTS))
        companions += [rope[..., half:], rope[..., :half]]
    return jnp.concatenate(blocks + companions, axis=-1).astype(jnp.bfloat16)


def _prep_w_kv(w_kv_b):
    depth, r, _ = w_kv_b.shape
    w = w_kv_b.reshape(depth, r, N_HEADS, QK_NOPE_DIM + V_HEAD_DIM)
    blocks = [_head_block(w[:, :, hd, :QK_NOPE_DIM], None, hd % N_ROPE_LAYOUTS) for hd in range(N_HEADS)]
    values = [w[:, :, hd, QK_NOPE_DIM:] for hd in range(N_HEADS)]
    return jnp.concatenate(blocks + values, axis=-1).astype(jnp.bfloat16)


def kernel(x, p, g_pre_mix, w_in, g_q_a, w_q_b, g_kv_a, w_kv_b, w_conv, g_attn_out, g_conv_out, w_out, g_post_mix, g_pre_mlp, w_mlp_up, w_mlp_down, g_post_mlp, g_pre_ple, w_ple_gate, w_ple_proj, g_post_ple):
    depth = w_in.shape[0]
    s = x.shape[1]
    tab = _rope_tables(s)
    win = _prep_w_in(w_in)
    wq = _prep_w_q(w_q_b)
    wkv = _prep_w_kv(w_kv_b)
    bf = lambda w: w.astype(jnp.bfloat16)
    wout, wup, wdn, wg, wp = bf(w_out), bf(w_mlp_up), bf(w_mlp_down), bf(w_ple_gate), bf(w_ple_proj)
    rows = lambda g: g[:, None, :]
    gains = {name: rows(g) for name, g in dict(
        pre_mix=g_pre_mix, q_a=g_q_a, kv_a=g_kv_a, attn_out=g_attn_out, conv_out=g_conv_out,
        post_mix=g_post_mix, pre_mlp=g_pre_mlp, post_mlp=g_post_mlp, pre_ple=g_pre_ple,
        post_ple=g_post_ple).items()}

    for i in range(depth):
        q, k, v, cn = _in_proj(i, x, tab, gains["pre_mix"], win, gains["q_a"], wq, gains["kv_a"], wkv,
                               w_conv, gains["conv_out"])
        an = _attention(i, q, k, v, gains["attn_out"])
        x = _tail(i, x, an, cn, p, wout, gains["post_mix"], gains["pre_mlp"], wup, wdn,
                  gains["post_mlp"], gains["pre_ple"], wg, wp, gains["post_ple"])
    return x
```

```python
import math

import jax
import jax.numpy as jnp
from jax.experimental import pallas as pl
from jax.experimental.pallas import tpu as pltpu

D_MODEL = 1024
N_HEADS = 8
V_HEAD_DIM = 64
QK_NOPE_DIM = 64
QK_ROPE_DIM = 32
QK_HEAD_DIM = QK_NOPE_DIM + QK_ROPE_DIM
Q_LORA_RANK = 384
KV_LORA_RANK = 256
CONV_CHANNELS = 512
ATTN_WIDTH = 512
GROUP_WIDTH = 64
D_FF = 4096
PLE_DIM = 256
ROPE_THETA = 10000.0
EPS = 1e-6

LANES = 128
SUBLANES = 8
ONES_ROWS = 16
HEAD_PAD = LANES
N_ROPE_LAYOUTS = LANES // QK_ROPE_DIM
QK_SCALE = math.log2(math.e) / math.sqrt(QK_HEAD_DIM)
HALO_ROWS = 8

C_QA = 0
C_KVA = C_QA + Q_LORA_RANK
C_KR = C_KVA + KV_LORA_RANK
C_GB = C_KR + HEAD_PAD
C_GC = C_GB + CONV_CHANNELS
C_CI = C_GC + CONV_CHANNELS
IN_EXT = C_CI + CONV_CHANNELS

TM_IN = 512
TQ = 512
CK = 256
TM_TAIL = 512
TAIL_SUB = 256
FF_CHUNK = 1024
PREP_ROWS = 256
VMEM_LIMIT = 56 * 1024 * 1024


def _rms(x, g):
    return x * jax.lax.rsqrt(jnp.mean(x * x, axis=-1, keepdims=True) + EPS) * g


def _group_rms_128(c, g):
    lane = jax.lax.broadcasted_iota(jnp.int32, c.shape, 1)
    lo = lane < GROUP_WIDTH
    sq = c * c
    s0 = jnp.sum(jnp.where(lo, sq, 0.0), axis=-1, keepdims=True)
    s1 = jnp.sum(jnp.where(lo, 0.0, sq), axis=-1, keepdims=True)
    r0 = jax.lax.rsqrt(s0 * (1.0 / GROUP_WIDTH) + EPS)
    r1 = jax.lax.rsqrt(s1 * (1.0 / GROUP_WIDTH) + EPS)
    return c * jnp.where(lo, r0, r1) * g


def _dot(a, b):
    return jnp.dot(a, b, preferred_element_type=jnp.float32)


def _rope_window(layout):
    lane = jax.lax.broadcasted_iota(jnp.int32, (1, LANES), 1)
    lo = layout * QK_ROPE_DIM
    return (lane >= lo) & (lane < lo + QK_ROPE_DIM)


def _in_proj_kernel(x_ref, xp_ref, xn_ref, tab_ref, gpre_ref, win_ref, gqa_ref, wq_ref,
                    gkva_ref, wkv_ref, wconv_ref, gconv_ref,
                    q_ref, k_ref, v_ref, cn_ref):
    t = pl.program_id(1)
    nt = pl.num_programs(1)
    tm = x_ref.shape[1]
    gpre = gpre_ref[...]

    h = _rms(x_ref[0], gpre).astype(jnp.bfloat16)
    z_att = _dot(h, win_ref[:, :C_GB])
    z_cv = _dot(h, win_ref[:, C_GB:])

    tab = tab_ref[...]
    cos_q, sin_q = tab[:, 0:LANES], tab[:, LANES:2 * LANES]
    cos_k, sin_k = tab[:, 2 * LANES:3 * LANES], tab[:, 3 * LANES:4 * LANES]
    windows = [_rope_window(r) for r in range(N_ROPE_LAYOUTS)]

    qa = _rms(z_att[:, C_QA:C_KVA], gqa_ref[...]).astype(jnp.bfloat16)
    kva = _rms(z_att[:, C_KVA:C_KR], gkva_ref[...]).astype(jnp.bfloat16)
    zkr = z_att[:, C_KR:C_GB]
    krr = zkr * cos_k + pltpu.roll(zkr, QK_ROPE_DIM // 2, 1) * sin_k
    gb = z_cv[:, :CONV_CHANNELS]
    u = z_cv[:, CONV_CHANNELS:2 * CONV_CHANNELS] * z_cv[:, 2 * CONV_CHANNELS:]
    qq = _dot(qa, wq_ref[...])
    kv = _dot(kva, wkv_ref[...])

    q_cos = [jnp.where(w, cos_q, QK_SCALE) for w in windows]
    q_sin = [jnp.where(w, sin_q, 0.0) for w in windows]
    kr_win = [jnp.where(w, krr, 0.0) for w in windows]
    for hd in range(N_HEADS):
        r = hd % N_ROPE_LAYOUTS
        lo = N_HEADS * HEAD_PAD + (hd // N_ROPE_LAYOUTS) * LANES
        qh = qq[:, hd * HEAD_PAD:(hd + 1) * HEAD_PAD] * q_cos[r] + qq[:, lo:lo + LANES] * q_sin[r]
        q_ref[0, hd] = qh.T.astype(q_ref.dtype)
        k_ref[0, hd] = (kv[:, hd * HEAD_PAD:(hd + 1) * HEAD_PAD] + kr_win[r]).astype(k_ref.dtype)
    for j in range(N_HEADS // 2):
        lo = N_HEADS * HEAD_PAD + j * LANES
        v_ref[0, j] = kv[:, lo:lo + LANES].T.astype(v_ref.dtype)

    xh = jnp.concatenate([xp_ref[0], xn_ref[0]], axis=0)
    zh = _dot(_rms(xh, gpre).astype(jnp.bfloat16), win_ref[:, C_GC:])
    uh = zh[:, :CONV_CHANNELS] * zh[:, CONV_CHANNELS:]
    u_first = jnp.where(t > 0, uh[HALO_ROWS - 1:HALO_ROWS], 0.0)
    u_last = jnp.where(t < nt - 1, uh[HALO_ROWS:HALO_ROWS + 1], 0.0)
    r_idx = jax.lax.broadcasted_iota(jnp.int32, u.shape, 0)
    u_dn = jnp.where(r_idx == 0, u_first, pltpu.roll(u, 1, 0))
    u_up = jnp.where(r_idx == tm - 1, u_last, pltpu.roll(u, tm - 1, 0))
    wc = wconv_ref[...]
    cv = gb * (u_dn * wc[0:1] + u * wc[1:2] + u_up * wc[2:3])
    gconv = gconv_ref[...]
    for j in range(CONV_CHANNELS // LANES):
        sl = slice(j * LANES, (j + 1) * LANES)
        cn_ref[0, :, sl] = _group_rms_128(cv[:, sl], gconv[:, sl]).astype(cn_ref.dtype)


def _layer_spec(a, layer):
    return pl.BlockSpec((None,) + a.shape[1:], lambda *_: (layer, 0, 0))


def _in_proj(layer, x, tab, gpre, win, gqa, wq, gkva, wkv, wconv, gconv):
    b, s, d = x.shape
    tm = TM_IN
    nt = s // tm
    hb = tm // HALO_ROWS
    n_halo = s // HALO_ROWS
    rows_out = lambda bi, ti: (bi, 0, ti, 0)
    cols_out = lambda bi, ti: (bi, 0, 0, ti)
    return pl.pallas_call(
        _in_proj_kernel,
        grid=(b, nt),
        in_specs=[
            pl.BlockSpec((1, tm, d), lambda bi, ti: (bi, ti, 0)),
            pl.BlockSpec((1, HALO_ROWS, d), lambda bi, ti: (bi, jnp.maximum(ti * hb - 1, 0), 0)),
            pl.BlockSpec((1, HALO_ROWS, d), lambda bi, ti: (bi, jnp.minimum((ti + 1) * hb, n_halo - 1), 0)),
            pl.BlockSpec((tm, tab.shape[1]), lambda bi, ti: (ti, 0)),
        ] + [_layer_spec(a, layer) for a in (gpre, win, gqa, wq, gkva, wkv, wconv, gconv)],
        out_specs=[
            pl.BlockSpec((1, N_HEADS, HEAD_PAD, tm), cols_out),
            pl.BlockSpec((1, N_HEADS, tm, HEAD_PAD), rows_out),
            pl.BlockSpec((1, N_HEADS // 2, LANES, tm), cols_out),
            pl.BlockSpec((1, tm, CONV_CHANNELS), lambda bi, ti: (bi, ti, 0)),
        ],
        out_shape=[
            jax.ShapeDtypeStruct((b, N_HEADS, HEAD_PAD, s), jnp.bfloat16),
            jax.ShapeDtypeStruct((b, N_HEADS, s, HEAD_PAD), jnp.bfloat16),
            jax.ShapeDtypeStruct((b, N_HEADS // 2, LANES, s), jnp.bfloat16),
            jax.ShapeDtypeStruct((b, s, CONV_CHANNELS), jnp.bfloat16),
        ],
        compiler_params=pltpu.CompilerParams(
            dimension_semantics=("arbitrary", "arbitrary"), vmem_limit_bytes=VMEM_LIMIT),
        name="in_proj",
    )(x, x, x, tab, gpre, win, gqa, wq, gkva, wkv, wconv, gconv)


def _attention_kernel(q_ref, k_ref, v_ref, g_ref, o_ref, s_ref, m_ref):
    tq = q_ref.shape[3]
    s_len = k_ref.shape[2]
    n_chunks = s_len // CK

    @pl.when(pl.program_id(0) == 0)
    def _():
        s_ref[...] = jnp.zeros(s_ref.shape, s_ref.dtype)
        m_ref[...] = jnp.zeros(m_ref.shape, m_ref.dtype)

    ones = jnp.ones((ONES_ROWS, CK), jnp.bfloat16)
    m_prev = [m_ref[hh] for hh in range(2)]
    qt = [q_ref[0, hh] for hh in range(2)]
    acc = [jnp.zeros((V_HEAD_DIM + ONES_ROWS, tq), jnp.float32) for _ in range(2)]
    m8 = [None, None]
    for c in range(n_chunks):
        rows = slice(c * CK, (c + 1) * CK)
        for hh in range(2):
            pt = jnp.exp2(s_ref[hh, rows, :] - m_prev[hh]).astype(jnp.bfloat16)
            vt = v_ref[0, 0, hh * V_HEAD_DIM:(hh + 1) * V_HEAD_DIM, rows]
            acc[hh] = acc[hh] + _dot(jnp.concatenate([vt, ones], axis=0), pt)
            st = _dot(k_ref[0, hh, rows, :], qt[hh])
            s_ref[hh, rows, :] = st
            cm = jnp.max(st.reshape(CK // SUBLANES, SUBLANES, tq), axis=0)
            m8[hh] = cm if m8[hh] is None else jnp.maximum(m8[hh], cm)
    outs = []
    for hh in range(2):
        m_ref[hh] = jnp.max(m8[hh], axis=0, keepdims=True)
        o = acc[hh][:V_HEAD_DIM] / acc[hh][V_HEAD_DIM:V_HEAD_DIM + 1]
        ssq = jnp.sum(o * o, axis=0, keepdims=True)
        outs.append(o * jax.lax.rsqrt(ssq * (1.0 / V_HEAD_DIM) + EPS))
    o_pair = jnp.concatenate(outs, axis=0).T
    o_ref[0] = (o_pair * g_ref[...]).astype(o_ref.dtype)


def _attention(layer, q, k, v, g_attn):
    b, nh, s, _ = k.shape
    n_pairs = nh // 2
    nq = s // TQ
    n_tiles = b * n_pairs * nq

    def tile_index(t):
        return t // (n_pairs * nq), (t // nq) % n_pairs, t % nq

    def scored(t):
        return tile_index(jnp.minimum(t, n_tiles - 1))

    def finished(t):
        return tile_index(jnp.maximum(t - 1, 0))

    def q_map(t):
        bi, j, qi = scored(t)
        return bi, j, 0, qi

    def k_map(t):
        bi, j, _ = scored(t)
        return bi, j, 0, 0

    def v_map(t):
        bi, j, _ = finished(t)
        return bi, j, 0, 0

    def g_map(t):
        return layer, 0, finished(t)[1]

    def o_map(t):
        bi, j, qi = finished(t)
        return bi, qi, j

    return pl.pallas_call(
        _attention_kernel,
        grid=(n_tiles + 1,),
        in_specs=[
            pl.BlockSpec((1, 2, HEAD_PAD, TQ), q_map),
            pl.BlockSpec((1, 2, s, HEAD_PAD), k_map),
            pl.BlockSpec((1, 1, LANES, s), v_map),
            pl.BlockSpec((None, 1, LANES), g_map),
        ],
        out_specs=pl.BlockSpec((1, TQ, LANES), o_map),
        out_shape=jax.ShapeDtypeStruct((b, s, ATTN_WIDTH), jnp.bfloat16),
        scratch_shapes=[pltpu.VMEM((2, s, TQ), jnp.float32), pltpu.VMEM((2, 1, TQ), jnp.float32)],
        compiler_params=pltpu.CompilerParams(
            dimension_semantics=("arbitrary",), vmem_limit_bytes=VMEM_LIMIT),
        name="attention",
    )(q, k, v, g_attn)


def _tail_kernel(x_ref, an_ref, cn_ref, p_ref, wout_ref, gpost_ref, gpmlp_ref, wup_ref, wdn_ref,
                 gomlp_ref, gpple_ref, wg_ref, wp_ref, gople_ref, o_ref):
    n_sub = x_ref.shape[1] // TAIL_SUB
    rows = [slice(i * TAIL_SUB, (i + 1) * TAIL_SUB) for i in range(n_sub)]
    st = [dict() for _ in range(n_sub)]

    def out_proj(i):
        st[i]["mix"] = (_dot(an_ref[0, rows[i], :], wout_ref[:ATTN_WIDTH, :])
                        + _dot(cn_ref[0, rows[i], :], wout_ref[ATTN_WIDTH:, :]))

    def post_mix(i):
        x = x_ref[0, rows[i], :] + _rms(st[i].pop("mix"), gpost_ref[...])
        st[i]["x"] = x
        st[i]["h"] = _rms(x, gpmlp_ref[...]).astype(jnp.bfloat16)

    def mlp(i):
        hm = st[i].pop("h")
        f = None
        for j in range(D_FF // FF_CHUNK):
            sl = slice(j * FF_CHUNK, (j + 1) * FF_CHUNK)
            a = jnp.maximum(_dot(hm, wup_ref[:, sl]), 0.0)
            d = _dot((a * a).astype(jnp.bfloat16), wdn_ref[sl, :])
            f = d if f is None else f + d
        st[i]["f"] = f

    def post_mlp(i):
        x = st[i]["x"] + _rms(st[i].pop("f"), gomlp_ref[...])
        st[i]["x"] = x
        st[i]["h"] = _rms(x, gpple_ref[...]).astype(jnp.bfloat16)

    def ple_proj(i):
        st[i]["g"] = _dot(st[i].pop("h"), wg_ref[...])
        st[i]["e"] = _dot(p_ref[0, rows[i], :].astype(jnp.bfloat16), wp_ref[...])

    def post_ple(i):
        gate = jax.nn.sigmoid(st[i].pop("g"))
        o_ref[0, rows[i], :] = st[i].pop("x") + _rms(gate * st[i].pop("e"), gople_ref[...])

    stages = [out_proj, post_mix, mlp, post_mlp, ple_proj, post_ple]
    for step in range(len(stages) + n_sub - 1):
        for i in range(n_sub):
            if 0 <= step - i < len(stages):
                stages[step - i](i)


def _tail(layer, x, an, cn, p, wout, gpost, gpmlp, wup, wdn, gomlp, gpple, wg, wp, gople):
    b, s, d = x.shape
    tm = TM_TAIL
    tile = lambda w: pl.BlockSpec((1, tm, w), lambda bi, ti: (bi, ti, 0))
    resident = lambda a: pl.BlockSpec((None,) + a.shape[1:], lambda bi, ti: (layer, 0, 0),
                                      pipeline_mode=pl.Buffered(1))
    return pl.pallas_call(
        _tail_kernel,
        grid=(b, s // tm),
        in_specs=[tile(d), tile(ATTN_WIDTH), tile(CONV_CHANNELS),
                  pl.BlockSpec((None, 1, tm, PLE_DIM), lambda bi, ti: (layer, bi, ti, 0)),
                  resident(wout), resident(gpost), resident(gpmlp), resident(wup), resident(wdn),
                  resident(gomlp), resident(gpple), resident(wg), resident(wp), resident(gople)],
        out_specs=tile(d),
        out_shape=jax.ShapeDtypeStruct(x.shape, x.dtype),
        compiler_params=pltpu.CompilerParams(
            dimension_semantics=("arbitrary", "arbitrary"), vmem_limit_bytes=VMEM_LIMIT),
        name="tail",
    )(x, an, cn, p, wout, gpost, gpmlp, wup, wdn, gomlp, gpple, wg, wp, gople)


def _rope_tables(seq):
    inv_freq = 1.0 / (ROPE_THETA ** (jnp.arange(0, QK_ROPE_DIM, 2, dtype=jnp.float32) / QK_ROPE_DIM))
    ang = jnp.arange(seq, dtype=jnp.float32)[:, None] * inv_freq[None, :]
    cos, sin = jnp.cos(ang), jnp.sin(ang)
    reps = LANES // QK_ROPE_DIM
    cos_rep = jnp.tile(jnp.concatenate([cos, cos], axis=1), (1, reps))
    sin_rep = jnp.tile(jnp.concatenate([-sin, sin], axis=1), (1, reps))
    return jnp.concatenate([QK_SCALE * cos_rep, QK_SCALE * sin_rep, cos_rep, sin_rep], axis=1)


def _head_block(nope, rope, layout):
    z32 = jnp.zeros(nope.shape[:-1] + (QK_ROPE_DIM,), nope.dtype)
    rope = z32 if rope is None else rope
    n = QK_ROPE_DIM
    pieces = {0: [rope, nope, z32],
              1: [nope[..., :n], rope, nope[..., n:], z32],
              2: [nope, rope, z32],
              3: [nope, z32, rope]}[layout]
    return jnp.concatenate(pieces, axis=-1)


def _prep_w_in_kernel(w_ref, o_ref):
    x = w_ref[0]
    o_ref[0, :, :C_KR] = x[:, :C_KR]
    blk = x[:, C_KR:C_KR + LANES].astype(jnp.float32)
    lane = jax.lax.broadcasted_iota(jnp.int32, blk.shape, 1)
    kr = jnp.where(lane < QK_ROPE_DIM, blk, 0.0)
    rep = kr
    for i in range(1, LANES // QK_ROPE_DIM):
        rep = rep + pltpu.roll(kr, i * QK_ROPE_DIM, 1)
    o_ref[0, :, C_KR:C_GB] = rep.astype(o_ref.dtype)
    o_ref[0, :, C_GB:] = x[:, C_KR + QK_ROPE_DIM:]


def _prep_w_in(w_in):
    depth, d, n = w_in.shape
    return pl.pallas_call(
        _prep_w_in_kernel,
        grid=(depth, d // PREP_ROWS),
        in_specs=[pl.BlockSpec((1, PREP_ROWS, n), lambda i, r: (i, r, 0))],
        out_specs=pl.BlockSpec((1, PREP_ROWS, IN_EXT), lambda i, r: (i, r, 0)),
        out_shape=jax.ShapeDtypeStruct((depth, d, IN_EXT), jnp.bfloat16),
        compiler_params=pltpu.CompilerParams(dimension_semantics=("arbitrary", "arbitrary")),
        name="prep_w_in",
    )(w_in.astype(jnp.bfloat16))


def _prep_w_q(w_q_b):
    depth, r, _ = w_q_b.shape
    half = QK_ROPE_DIM // 2
    w = w_q_b.reshape(depth, r, N_HEADS, QK_HEAD_DIM)
    blocks, companions = [], []
    for hd in range(N_HEADS):
        nope, rope = w[:, :, hd, :QK_NOPE_DIM], w[:, :, hd, QK_NOPE_DIM:]
        blocks.append(_head_block(nope, rope, hd % N_ROPE_LAYOUTS))
        companions += [rope[..., half:], rope[..., :half]]
    return jnp.concatenate(blocks + companions, axis=-1).astype(jnp.bfloat16)


def _prep_w_kv(w_kv_b):
    depth, r, _ = w_kv_b.shape
    w = w_kv_b.reshape(depth, r, N_HEADS, QK_NOPE_DIM + V_HEAD_DIM)
    blocks = [_head_block(w[:, :, hd, :QK_NOPE_DIM], None, hd % N_ROPE_LAYOUTS) for hd in range(N_HEADS)]
    values = [w[:, :, hd, QK_NOPE_DIM:] for hd in range(N_HEADS)]
    return jnp.concatenate(blocks + values, axis=-1).astype(jnp.bfloat16)


def kernel(x, p, g_pre_mix, w_in, g_q_a, w_q_b, g_kv_a, w_kv_b, w_conv, g_attn_out, g_conv_out, w_out, g_post_mix, g_pre_mlp, w_mlp_up, w_mlp_down, g_post_mlp, g_pre_ple, w_ple_gate, w_ple_proj, g_post_ple):
    depth = w_in.shape[0]
    s = x.shape[1]
    tab = _rope_tables(s)
    win = _prep_w_in(w_in)
    wq = _prep_w_q(w_q_b)
    wkv = _prep_w_kv(w_kv_b)
    bf = lambda w: w.astype(jnp.bfloat16)
    wout, wup, wdn, wg, wp = bf(w_out), bf(w_mlp_up), bf(w_mlp_down), bf(w_ple_gate), bf(w_ple_proj)
    rows = lambda g: g[:, None, :]
    gains = {name: rows(g) for name, g in dict(
        pre_mix=g_pre_mix, q_a=g_q_a, kv_a=g_kv_a, attn_out=g_attn_out, conv_out=g_conv_out,
        post_mix=g_post_mix, pre_mlp=g_pre_mlp, post_mlp=g_post_mlp, pre_ple=g_pre_ple,
        post_ple=g_post_ple).items()}

    for i in range(depth):
        q, k, v, cn = _in_proj(i, x, tab, gains["pre_mix"], win, gains["q_a"], wq, gains["kv_a"], wkv,
                               w_conv, gains["conv_out"])
        an = _attention(i, q, k, v, gains["attn_out"])
        x = _tail(i, x, an, cn, p, wout, gains["post_mix"], gains["pre_mlp"], wup, wdn,
                  gains["post_mlp"], gains["pre_ple"], wg, wp, gains["post_ple"])
    return x
```

```python
import math

import jax
import jax.numpy as jnp
import numpy as np
from jax.experimental import pallas as pl
from jax.experimental.pallas import tpu as pltpu

D_MODEL = 1024
N_HEADS = 8
V_HEAD_DIM = 64
QK_NOPE_DIM = 64
QK_ROPE_DIM = 32
QK_HEAD_DIM = QK_NOPE_DIM + QK_ROPE_DIM
Q_LORA_RANK = 384
KV_LORA_RANK = 256
CONV_CHANNELS = 512
ATTN_WIDTH = 512
GROUP_WIDTH = 64
D_FF = 4096
PLE_DIM = 256
ROPE_THETA = 10000.0
EPS = 1e-6

LANES = 128
SUBLANES = 8
ONES_ROWS = 16
HEAD_PAD = LANES
N_ROPE_LAYOUTS = LANES // QK_ROPE_DIM
QK_SCALE = math.log2(math.e) / math.sqrt(QK_HEAD_DIM)
HALO_ROWS = 8

C_QA = 0
C_KVA = C_QA + Q_LORA_RANK
C_KR = C_KVA + KV_LORA_RANK
C_GB = C_KR + HEAD_PAD
C_GC = C_GB + CONV_CHANNELS
C_CI = C_GC + CONV_CHANNELS
IN_EXT = C_CI + CONV_CHANNELS

TM_IN = 512
TQ = 512
CK = 256
TM_TAIL = 512
TAIL_SUB = 256
FF_CHUNK = 1024
PREP_ROWS = 256
VMEM_LIMIT = 56 * 1024 * 1024


def _rms(x, g):
    return x * jax.lax.rsqrt(jnp.mean(x * x, axis=-1, keepdims=True) + EPS) * g


def _group_rms_128(c, g):
    lane = jax.lax.broadcasted_iota(jnp.int32, c.shape, 1)
    lo = lane < GROUP_WIDTH
    sq = c * c
    s0 = jnp.sum(jnp.where(lo, sq, 0.0), axis=-1, keepdims=True)
    s1 = jnp.sum(jnp.where(lo, 0.0, sq), axis=-1, keepdims=True)
    r0 = jax.lax.rsqrt(s0 * (1.0 / GROUP_WIDTH) + EPS)
    r1 = jax.lax.rsqrt(s1 * (1.0 / GROUP_WIDTH) + EPS)
    return c * jnp.where(lo, r0, r1) * g


def _dot(a, b):
    return jnp.dot(a, b, preferred_element_type=jnp.float32)


def _rope_window(layout):
    lane = jax.lax.broadcasted_iota(jnp.int32, (1, LANES), 1)
    lo = layout * QK_ROPE_DIM
    return (lane >= lo) & (lane < lo + QK_ROPE_DIM)


def _in_proj_kernel(x_ref, xp_ref, xn_ref, tab_ref, gpre_ref, win_ref, gqa_ref, wq_ref,
                    gkva_ref, wkv_ref, wconv_ref, gconv_ref,
                    q_ref, k_ref, v_ref, cn_ref):
    t = pl.program_id(1)
    nt = pl.num_programs(1)
    tm = x_ref.shape[1]
    gpre = gpre_ref[...]

    h = _rms(x_ref[0], gpre).astype(jnp.bfloat16)
    z_att = _dot(h, win_ref[:, :C_GB])
    z_cv = _dot(h, win_ref[:, C_GB:])

    tab = tab_ref[...]
    cos_q, sin_q = tab[:, 0:LANES], tab[:, LANES:2 * LANES]
    cos_k, sin_k = tab[:, 2 * LANES:3 * LANES], tab[:, 3 * LANES:4 * LANES]
    windows = [_rope_window(r) for r in range(N_ROPE_LAYOUTS)]

    qa = _rms(z_att[:, C_QA:C_KVA], gqa_ref[...]).astype(jnp.bfloat16)
    kva = _rms(z_att[:, C_KVA:C_KR], gkva_ref[...]).astype(jnp.bfloat16)
    zkr = z_att[:, C_KR:C_GB]
    krr = zkr * cos_k + pltpu.roll(zkr, QK_ROPE_DIM // 2, 1) * sin_k
    gb = z_cv[:, :CONV_CHANNELS]
    u = z_cv[:, CONV_CHANNELS:2 * CONV_CHANNELS] * z_cv[:, 2 * CONV_CHANNELS:]
    qq = _dot(qa, wq_ref[...])
    kv = _dot(kva, wkv_ref[...])

    q_cos = [jnp.where(w, cos_q, QK_SCALE) for w in windows]
    q_sin = [jnp.where(w, sin_q, 0.0) for w in windows]
    kr_win = [jnp.where(w, krr, 0.0) for w in windows]
    for hd in range(N_HEADS):
        r = hd % N_ROPE_LAYOUTS
        lo = N_HEADS * HEAD_PAD + (hd // N_ROPE_LAYOUTS) * LANES
        qh = qq[:, hd * HEAD_PAD:(hd + 1) * HEAD_PAD] * q_cos[r] + qq[:, lo:lo + LANES] * q_sin[r]
        q_ref[0, hd] = qh.T.astype(q_ref.dtype)
        k_ref[0, hd] = (kv[:, hd * HEAD_PAD:(hd + 1) * HEAD_PAD] + kr_win[r]).astype(k_ref.dtype)
    for j in range(N_HEADS // 2):
        lo = N_HEADS * HEAD_PAD + j * LANES
        v_ref[0, j] = kv[:, lo:lo + LANES].T.astype(v_ref.dtype)

    xh = jnp.concatenate([xp_ref[0], xn_ref[0]], axis=0)
    zh = _dot(_rms(xh, gpre).astype(jnp.bfloat16), win_ref[:, C_GC:])
    uh = zh[:, :CONV_CHANNELS] * zh[:, CONV_CHANNELS:]
    u_first = jnp.where(t > 0, uh[HALO_ROWS - 1:HALO_ROWS], 0.0)
    u_last = jnp.where(t < nt - 1, uh[HALO_ROWS:HALO_ROWS + 1], 0.0)
    r_idx = jax.lax.broadcasted_iota(jnp.int32, u.shape, 0)
    u_dn = jnp.where(r_idx == 0, u_first, pltpu.roll(u, 1, 0))
    u_up = jnp.where(r_idx == tm - 1, u_last, pltpu.roll(u, tm - 1, 0))
    wc = wconv_ref[...]
    cv = gb * (u_dn * wc[0:1] + u * wc[1:2] + u_up * wc[2:3])
    gconv = gconv_ref[...]
    for j in range(CONV_CHANNELS // LANES):
        sl = slice(j * LANES, (j + 1) * LANES)
        cn_ref[0, :, sl] = _group_rms_128(cv[:, sl], gconv[:, sl]).astype(cn_ref.dtype)


def _layer_spec(a, layer):
    return pl.BlockSpec((None,) + a.shape[1:], lambda *_: (layer, 0, 0))


def _in_proj(layer, x, tab, gpre, win, gqa, wq, gkva, wkv, wconv, gconv):
    b, s, d = x.shape
    tm = TM_IN
    nt = s // tm
    hb = tm // HALO_ROWS
    n_halo = s // HALO_ROWS
    rows_out = lambda bi, ti: (bi, 0, ti, 0)
    cols_out = lambda bi, ti: (bi, 0, 0, ti)
    return pl.pallas_call(
        _in_proj_kernel,
        grid=(b, nt),
        in_specs=[
            pl.BlockSpec((1, tm, d), lambda bi, ti: (bi, ti, 0)),
            pl.BlockSpec((1, HALO_ROWS, d), lambda bi, ti: (bi, jnp.maximum(ti * hb - 1, 0), 0)),
            pl.BlockSpec((1, HALO_ROWS, d), lambda bi, ti: (bi, jnp.minimum((ti + 1) * hb, n_halo - 1), 0)),
            pl.BlockSpec((tm, tab.shape[1]), lambda bi, ti: (ti, 0)),
        ] + [_layer_spec(a, layer) for a in (gpre, win, gqa, wq, gkva, wkv, wconv, gconv)],
        out_specs=[
            pl.BlockSpec((1, N_HEADS, HEAD_PAD, tm), cols_out),
            pl.BlockSpec((1, N_HEADS, tm, HEAD_PAD), rows_out),
            pl.BlockSpec((1, N_HEADS // 2, LANES, tm), cols_out),
            pl.BlockSpec((1, tm, CONV_CHANNELS), lambda bi, ti: (bi, ti, 0)),
        ],
        out_shape=[
            jax.ShapeDtypeStruct((b, N_HEADS, HEAD_PAD, s), jnp.bfloat16),
            jax.ShapeDtypeStruct((b, N_HEADS, s, HEAD_PAD), jnp.bfloat16),
            jax.ShapeDtypeStruct((b, N_HEADS // 2, LANES, s), jnp.bfloat16),
            jax.ShapeDtypeStruct((b, s, CONV_CHANNELS), jnp.bfloat16),
        ],
        compiler_params=pltpu.CompilerParams(
            dimension_semantics=("arbitrary", "arbitrary"), vmem_limit_bytes=VMEM_LIMIT),
        name="in_proj",
    )(x, x, x, tab, gpre, win, gqa, wq, gkva, wkv, wconv, gconv)


def _attention_kernel(q_ref, k_ref, v_ref, g_ref, o_ref, s_ref, m_ref):
    tq = q_ref.shape[3]
    s_len = k_ref.shape[2]
    n_chunks = s_len // CK

    @pl.when(pl.program_id(0) == 0)
    def _():
        s_ref[...] = jnp.zeros(s_ref.shape, s_ref.dtype)
        m_ref[...] = jnp.zeros(m_ref.shape, m_ref.dtype)

    ones = jnp.ones((ONES_ROWS, CK), jnp.bfloat16)
    m_prev = [m_ref[hh] for hh in range(2)]
    qt = [q_ref[0, hh] for hh in range(2)]
    acc = [jnp.zeros((V_HEAD_DIM + ONES_ROWS, tq), jnp.float32) for _ in range(2)]
    m8 = [None, None]
    for c in range(n_chunks):
        rows = slice(c * CK, (c + 1) * CK)
        for hh in range(2):
            pt = jnp.exp2(s_ref[hh, rows, :] - m_prev[hh]).astype(jnp.bfloat16)
            vt = v_ref[0, 0, hh * V_HEAD_DIM:(hh + 1) * V_HEAD_DIM, rows]
            acc[hh] = acc[hh] + _dot(jnp.concatenate([vt, ones], axis=0), pt)
            st = _dot(k_ref[0, hh, rows, :], qt[hh])
            s_ref[hh, rows, :] = st
            cm = jnp.max(st.reshape(CK // SUBLANES, SUBLANES, tq), axis=0)
            m8[hh] = cm if m8[hh] is None else jnp.maximum(m8[hh], cm)
    outs = []
    for hh in range(2):
        m_ref[hh] = jnp.max(m8[hh], axis=0, keepdims=True)
        o = acc[hh][:V_HEAD_DIM] / acc[hh][V_HEAD_DIM:V_HEAD_DIM + 1]
        ssq = jnp.sum(o * o, axis=0, keepdims=True)
        outs.append(o * jax.lax.rsqrt(ssq * (1.0 / V_HEAD_DIM) + EPS))
    o_pair = jnp.concatenate(outs, axis=0).T
    o_ref[0] = (o_pair * g_ref[...]).astype(o_ref.dtype)


def _attention(layer, q, k, v, g_attn):
    b, nh, s, _ = k.shape
    n_pairs = nh // 2
    nq = s // TQ
    n_tiles = b * n_pairs * nq

    def tile_index(t):
        return t // (n_pairs * nq), (t // nq) % n_pairs, t % nq

    def scored(t):
        return tile_index(jnp.minimum(t, n_tiles - 1))

    def finished(t):
        return tile_index(jnp.maximum(t - 1, 0))

    def q_map(t):
        bi, j, qi = scored(t)
        return bi, j, 0, qi

    def k_map(t):
        bi, j, _ = scored(t)
        return bi, j, 0, 0

    def v_map(t):
        bi, j, _ = finished(t)
        return bi, j, 0, 0

    def g_map(t):
        return layer, 0, finished(t)[1]

    def o_map(t):
        bi, j, qi = finished(t)
        return bi, qi, j

    return pl.pallas_call(
        _attention_kernel,
        grid=(n_tiles + 1,),
        in_specs=[
            pl.BlockSpec((1, 2, HEAD_PAD, TQ), q_map),
            pl.BlockSpec((1, 2, s, HEAD_PAD), k_map),
            pl.BlockSpec((1, 1, LANES, s), v_map),
            pl.BlockSpec((None, 1, LANES), g_map),
        ],
        out_specs=pl.BlockSpec((1, TQ, LANES), o_map),
        out_shape=jax.ShapeDtypeStruct((b, s, ATTN_WIDTH), jnp.bfloat16),
        scratch_shapes=[pltpu.VMEM((2, s, TQ), jnp.float32), pltpu.VMEM((2, 1, TQ), jnp.float32)],
        compiler_params=pltpu.CompilerParams(
            dimension_semantics=("arbitrary",), vmem_limit_bytes=VMEM_LIMIT),
        name="attention",
    )(q, k, v, g_attn)


def _tail_kernel(x_ref, an_ref, cn_ref, p_ref, wout_ref, gpost_ref, gpmlp_ref, wup_ref, wdn_ref,
                 gomlp_ref, gpple_ref, wg_ref, wp_ref, gople_ref, o_ref):
    n_sub = x_ref.shape[1] // TAIL_SUB
    rows = [slice(i * TAIL_SUB, (i + 1) * TAIL_SUB) for i in range(n_sub)]
    st = [dict() for _ in range(n_sub)]

    def out_proj(i):
        st[i]["mix"] = (_dot(an_ref[0, rows[i], :], wout_ref[:ATTN_WIDTH, :])
                        + _dot(cn_ref[0, rows[i], :], wout_ref[ATTN_WIDTH:, :]))

    def post_mix(i):
        x = x_ref[0, rows[i], :] + _rms(st[i].pop("mix"), gpost_ref[...])
        st[i]["x"] = x
        st[i]["h"] = _rms(x, gpmlp_ref[...]).astype(jnp.bfloat16)

    def mlp(i):
        hm = st[i].pop("h")
        f = None
        for j in range(D_FF // FF_CHUNK):
            sl = slice(j * FF_CHUNK, (j + 1) * FF_CHUNK)
            a = jnp.maximum(_dot(hm, wup_ref[:, sl]), 0.0)
            d = _dot((a * a).astype(jnp.bfloat16), wdn_ref[sl, :])
            f = d if f is None else f + d
        st[i]["f"] = f

    def post_mlp(i):
        x = st[i]["x"] + _rms(st[i].pop("f"), gomlp_ref[...])
        st[i]["x"] = x
        st[i]["h"] = _rms(x, gpple_ref[...]).astype(jnp.bfloat16)

    def ple_proj(i):
        st[i]["g"] = _dot(st[i].pop("h"), wg_ref[...])
        st[i]["e"] = _dot(p_ref[0, rows[i], :].astype(jnp.bfloat16), wp_ref[...])

    def post_ple(i):
        gate = jax.nn.sigmoid(st[i].pop("g"))
        o_ref[0, rows[i], :] = st[i].pop("x") + _rms(gate * st[i].pop("e"), gople_ref[...])

    stages = [out_proj, post_mix, mlp, post_mlp, ple_proj, post_ple]
    for step in range(len(stages) + n_sub - 1):
        for i in range(n_sub):
            if 0 <= step - i < len(stages):
                stages[step - i](i)


def _tail(layer, x, an, cn, p, wout, gpost, gpmlp, wup, wdn, gomlp, gpple, wg, wp, gople):
    b, s, d = x.shape
    tm = TM_TAIL
    tile = lambda w: pl.BlockSpec((1, tm, w), lambda bi, ti: (bi, ti, 0))
    resident = lambda a: pl.BlockSpec((None,) + a.shape[1:], lambda bi, ti: (layer, 0, 0),
                                      pipeline_mode=pl.Buffered(1))
    return pl.pallas_call(
        _tail_kernel,
        grid=(b, s // tm),
        in_specs=[tile(d), tile(ATTN_WIDTH), tile(CONV_CHANNELS),
                  pl.BlockSpec((None, 1, tm, PLE_DIM), lambda bi, ti: (layer, bi, ti, 0)),
                  resident(wout), resident(gpost), resident(gpmlp), resident(wup), resident(wdn),
                  resident(gomlp), resident(gpple), resident(wg), resident(wp), resident(gople)],
        out_specs=tile(d),
        out_shape=jax.ShapeDtypeStruct(x.shape, x.dtype),
        compiler_params=pltpu.CompilerParams(
            dimension_semantics=("arbitrary", "arbitrary"), vmem_limit_bytes=VMEM_LIMIT),
        name="tail",
    )(x, an, cn, p, wout, gpost, gpmlp, wup, wdn, gomlp, gpple, wg, wp, gople)


def _rope_tables(seq):
    inv_freq = 1.0 / (ROPE_THETA ** (jnp.arange(0, QK_ROPE_DIM, 2, dtype=jnp.float32) / QK_ROPE_DIM))
    ang = jnp.arange(seq, dtype=jnp.float32)[:, None] * inv_freq[None, :]
    cos, sin = jnp.cos(ang), jnp.sin(ang)
    reps = LANES // QK_ROPE_DIM
    cos_rep = jnp.tile(jnp.concatenate([cos, cos], axis=1), (1, reps))
    sin_rep = jnp.tile(jnp.concatenate([-sin, sin], axis=1), (1, reps))
    return jnp.concatenate([QK_SCALE * cos_rep, QK_SCALE * sin_rep, cos_rep, sin_rep], axis=1)


def _head_block(nope, rope, layout):
    z32 = [-1] * QK_ROPE_DIM
    rope = z32 if rope is None else rope
    n = QK_ROPE_DIM
    return {0: rope + nope + z32,
            1: nope[:n] + rope + nope[n:] + z32,
            2: nope + rope + z32,
            3: nope + z32 + rope}[layout]


def _place_columns(w, src_cols):
    place = np.zeros((w.shape[-1], len(src_cols)), np.float32)
    for j, c in enumerate(src_cols):
        if c >= 0:
            place[c, j] = 1.0
    out = jnp.einsum("drk,kn->drn", w.astype(jnp.bfloat16), jnp.asarray(place, jnp.bfloat16),
                     preferred_element_type=jnp.float32)
    return out.astype(jnp.bfloat16)


def _prep_w_in_kernel(w_ref, o_ref):
    x = w_ref[0]
    o_ref[0, :, :C_KR] = x[:, :C_KR]
    blk = x[:, C_KR:C_KR + LANES].astype(jnp.float32)
    lane = jax.lax.broadcasted_iota(jnp.int32, blk.shape, 1)
    kr = jnp.where(lane < QK_ROPE_DIM, blk, 0.0)
    rep = kr
    for i in range(1, LANES // QK_ROPE_DIM):
        rep = rep + pltpu.roll(kr, i * QK_ROPE_DIM, 1)
    o_ref[0, :, C_KR:C_GB] = rep.astype(o_ref.dtype)
    o_ref[0, :, C_GB:] = x[:, C_KR + QK_ROPE_DIM:]


def _prep_w_in(w_in):
    depth, d, n = w_in.shape
    return pl.pallas_call(
        _prep_w_in_kernel,
        grid=(depth, d // PREP_ROWS),
        in_specs=[pl.BlockSpec((1, PREP_ROWS, n), lambda i, r: (i, r, 0))],
        out_specs=pl.BlockSpec((1, PREP_ROWS, IN_EXT), lambda i, r: (i, r, 0)),
        out_shape=jax.ShapeDtypeStruct((depth, d, IN_EXT), jnp.bfloat16),
        compiler_params=pltpu.CompilerParams(dimension_semantics=("arbitrary", "arbitrary")),
        name="prep_w_in",
    )(w_in.astype(jnp.bfloat16))


def _prep_w_q(w_q_b):
    half = QK_ROPE_DIM // 2
    blocks, companions = [], []
    for hd in range(N_HEADS):
        cols = list(range(hd * QK_HEAD_DIM, (hd + 1) * QK_HEAD_DIM))
        nope, rope = cols[:QK_NOPE_DIM], cols[QK_NOPE_DIM:]
        blocks += _head_block(nope, rope, hd % N_ROPE_LAYOUTS)
        companions += rope[half:] + rope[:half]
    return _place_columns(w_q_b, blocks + companions)


def _prep_w_kv(w_kv_b):
    width = QK_NOPE_DIM + V_HEAD_DIM
    blocks, values = [], []
    for hd in range(N_HEADS):
        cols = list(range(hd * width, (hd + 1) * width))
        blocks += _head_block(cols[:QK_NOPE_DIM], None, hd % N_ROPE_LAYOUTS)
        values += cols[QK_NOPE_DIM:]
    return _place_columns(w_kv_b, blocks + values)


def kernel(x, p, g_pre_mix, w_in, g_q_a, w_q_b, g_kv_a, w_kv_b, w_conv, g_attn_out, g_conv_out, w_out, g_post_mix, g_pre_mlp, w_mlp_up, w_mlp_down, g_post_mlp, g_pre_ple, w_ple_gate, w_ple_proj, g_post_ple):
    depth = w_in.shape[0]
    s = x.shape[1]
    tab = _rope_tables(s)
    win = _prep_w_in(w_in)
    wq = _prep_w_q(w_q_b)
    wkv = _prep_w_kv(w_kv_b)
    bf = lambda w: w.astype(jnp.bfloat16)
    wout, wup, wdn, wg, wp = bf(w_out), bf(w_mlp_up), bf(w_mlp_down), bf(w_ple_gate), bf(w_ple_proj)
    rows = lambda g: g[:, None, :]
    gains = {name: rows(g) for name, g in dict(
        pre_mix=g_pre_mix, q_a=g_q_a, kv_a=g_kv_a, attn_out=g_attn_out, conv_out=g_conv_out,
        post_mix=g_post_mix, pre_mlp=g_pre_mlp, post_mlp=g_post_mlp, pre_ple=g_pre_ple,
        post_ple=g_post_ple).items()}

    for i in range(depth):
        q, k, v, cn = _in_proj(i, x, tab, gains["pre_mix"], win, gains["q_a"], wq, gains["kv_a"], wkv,
                               w_conv, gains["conv_out"])
        an = _attention(i, q, k, v, gains["attn_out"])
        x = _tail(i, x, an, cn, p, wout, gains["post_mix"], gains["pre_mlp"], wup, wdn,
                  gains["post_mlp"], gains["pre_ple"], wg, wp, gains["post_ple"])
    return x
```
